```python
import jax, jax.numpy as jnp
from jax import lax
import numpy as np

D_MODEL = 1024
BATCH = 16
SEQ = 2048
DEPTH = 4

N_A = DEPTH // 2
N_B = DEPTH - N_A

GLA_HEADS = 4
GLA_QK = D_MODEL // 2
GLA_V = D_MODEL
GLA_HEAD_K = GLA_QK // GLA_HEADS
GLA_HEAD_V = GLA_V // GLA_HEADS
GATE_RANK = 16
GATE_TAU = 16.0
GLA_CHUNK = 64
GLA_IN = 2 * GLA_QK + 2 * GLA_V + GATE_RANK

SB_HEADS = 16
SB_HEAD_DIM = D_MODEL // SB_HEADS
BLOCK_Q = 128

D_FF = 4 * D_MODEL

DEEPNORM_ALPHA = (2.0 * DEPTH) ** 0.25
DEEPNORM_BETA = (8.0 * DEPTH) ** -0.25
LN_EPS = 1e-5
RMS_EPS = 1e-6

kernel_name = "yoco_gla_stickbreaking_deepnorm"


def layer_norm(x, g, b):
    xf = x.astype(jnp.float32)
    mu = jnp.mean(xf, axis=-1, keepdims=True)
    var = jnp.mean(jnp.square(xf - mu), axis=-1, keepdims=True)
    y = (xf - mu) * lax.rsqrt(var + LN_EPS)
    return (y * g.astype(jnp.float32) + b.astype(jnp.float32)).astype(x.dtype)


def gla_chunked(q, k, v, log_a):
    B, S, H, dk = q.shape
    dv = v.shape[-1]
    n = S // GLA_CHUNK

    def to_chunks(t):
        return t.astype(jnp.float32).reshape(B, n, GLA_CHUNK, H, t.shape[-1]).transpose(0, 3, 1, 2, 4)

    qc = to_chunks(q) * (dk ** -0.5)
    kc = to_chunks(k)
    vc = to_chunks(v)
    b = jnp.cumsum(to_chunks(log_a), axis=3)
    b_last = b[:, :, :, -1:, :]
    q_dec = qc * jnp.exp(b)
    k_dec = kc * jnp.exp(-b)
    k_end = kc * jnp.exp(b_last - b)

    causal = jnp.tril(jnp.ones((GLA_CHUNK, GLA_CHUNK), dtype=bool))
    att = jnp.where(causal, jnp.einsum('bhncd,bhnsd->bhncs', q_dec, k_dec), 0.0)
    o_intra = jnp.einsum('bhncs,bhnse->bhnce', att, vc)

    u = jnp.einsum('bhnsd,bhnse->bhnde', k_end, vc)
    decay = jnp.exp(b_last[:, :, :, 0, :])

    def step(state, inp):
        d_n, u_n = inp
        return d_n[..., None] * state + u_n, state

    init = jnp.zeros((B, H, dk, dv), jnp.float32)
    _, s_prev = lax.scan(step, init, (jnp.moveaxis(decay, 2, 0), jnp.moveaxis(u, 2, 0)))
    s_prev = jnp.moveaxis(s_prev, 0, 2)
    o_inter = jnp.einsum('bhncd,bhnde->bhnce', q_dec, s_prev)
    o = o_intra + o_inter
    return o.transpose(0, 2, 3, 1, 4).reshape(B, S, H, dv)


def gla_mixer(x, w_in, w_a2, b_a, norm_g, w_o):
    B, S, _ = x.shape
    proj = x @ w_in
    q, k, v, g, a1 = jnp.split(
        proj, [GLA_QK, 2 * GLA_QK, 2 * GLA_QK + GLA_V, 2 * GLA_QK + 2 * GLA_V], axis=-1)
    log_a = jax.nn.log_sigmoid((a1 @ w_a2 + b_a).astype(jnp.float32)) / GATE_TAU
    hk = lambda t: t.reshape(B, S, GLA_HEADS, GLA_HEAD_K)
    o = gla_chunked(hk(q), hk(k), v.reshape(B, S, GLA_HEADS, GLA_HEAD_V), hk(log_a))
    o = o * lax.rsqrt(jnp.mean(jnp.square(o), axis=-1, keepdims=True) + RMS_EPS)
    o = o * norm_g.astype(jnp.float32)
    o = o * jax.nn.silu(g.astype(jnp.float32).reshape(B, S, GLA_HEADS, GLA_HEAD_V))
    return o.reshape(B, S, GLA_V).astype(x.dtype) @ w_o


def stick_breaking_attention(q, k, v):
    S = q.shape[1]
    scale = SB_HEAD_DIM ** -0.5
    outs = []
    for i in range(S // BLOCK_Q):
        kv_len = (i + 1) * BLOCK_Q
        q_blk = q[:, i * BLOCK_Q:kv_len].astype(jnp.float32)
        k_blk = k[:, :kv_len].astype(jnp.float32)
        v_blk = v[:, :kv_len].astype(jnp.float32)
        z = jnp.einsum('bqhd,bkhd->bhqk', q_blk, k_blk) * scale
        t_idx = i * BLOCK_Q + jnp.arange(BLOCK_Q)
        s_idx = jnp.arange(kv_len)
        mask = s_idx[None, :] < t_idx[:, None]
        log_1m = jnp.where(mask, jax.nn.log_sigmoid(-z), 0.0)
        rev = lax.cumsum(log_1m, axis=3, reverse=True)
        log_w = jax.nn.log_sigmoid(z) + (rev - log_1m)
        w = jnp.where(mask, jnp.exp(log_w), 0.0)
        outs.append(jnp.einsum('bhqk,bkhd->bqhd', w, v_blk))
    return jnp.concatenate(outs, axis=1)


def sq_relu_mlp(x, w_up, w_down):
    h = jax.nn.relu(x @ w_up)
    return (h * h) @ w_down


def setup_inputs(seed: int = 0) -> dict:
    key = jax.random.key(seed)
    ks = jax.random.split(key, 16)
    nrm = lambda k, shape, s: jax.random.normal(k, shape, jnp.float32) * s
    return {
        "x": nrm(ks[0], (BATCH, SEQ, D_MODEL), 1.0),
        "gla_w_in": nrm(ks[1], (N_A, D_MODEL, GLA_IN), D_MODEL ** -0.5),
        "gla_w_a2": nrm(ks[2], (N_A, GATE_RANK, GLA_QK), GATE_RANK ** -0.5),
        "gla_b_a": nrm(ks[3], (N_A, GLA_QK), 0.1),
        "gla_norm_g": 1.0 + nrm(ks[4], (N_A, GLA_HEAD_V), 0.02),
        "gla_w_o": nrm(ks[5], (N_A, GLA_V, D_MODEL), GLA_V ** -0.5 * DEEPNORM_BETA),
        "kv_w": nrm(ks[6], (D_MODEL, 2 * D_MODEL), D_MODEL ** -0.5),
        "sb_w_q": nrm(ks[7], (N_B, D_MODEL, D_MODEL), D_MODEL ** -0.5),
        "sb_w_o": nrm(ks[8], (N_B, D_MODEL, D_MODEL), D_MODEL ** -0.5 * DEEPNORM_BETA),
        "mlp_w_up": nrm(ks[9], (DEPTH, D_MODEL, D_FF), D_MODEL ** -0.5),
        "mlp_w_down": nrm(ks[10], (DEPTH, D_FF, D_MODEL), D_FF ** -0.5 * DEEPNORM_BETA),
        "ln_g": 1.0 + nrm(ks[11], (DEPTH, 2, D_MODEL), 0.02),
        "ln_b": nrm(ks[12], (DEPTH, 2, D_MODEL), 0.02),
    }


def reference(x, gla_w_in, gla_w_a2, gla_b_a, gla_norm_g, gla_w_o, kv_w, sb_w_q, sb_w_o,
              mlp_w_up, mlp_w_down, ln_g, ln_b):
    B, S, D = x.shape
    h = x
    k_sh = None
    v_sh = None
    for layer in range(DEPTH):
        if layer < N_A:
            mix = gla_mixer(h, gla_w_in[layer], gla_w_a2[layer], gla_b_a[layer],
                            gla_norm_g[layer], gla_w_o[layer])
        else:
            j = layer - N_A
            if j == 0:
                kv = h @ kv_w
                k_sh = kv[..., :D].reshape(B, S, SB_HEADS, SB_HEAD_DIM)
                v_sh = kv[..., D:].reshape(B, S, SB_HEADS, SB_HEAD_DIM)
            q = (h @ sb_w_q[j]).reshape(B, S, SB_HEADS, SB_HEAD_DIM)
            o = stick_breaking_attention(q, k_sh, v_sh)
            mix = o.reshape(B, S, D).astype(h.dtype) @ sb_w_o[j]
        h = layer_norm(DEEPNORM_ALPHA * h + mix, ln_g[layer, 0], ln_b[layer, 0])
        h = layer_norm(DEEPNORM_ALPHA * h + sq_relu_mlp(h, mlp_w_up[layer], mlp_w_down[layer]),
                       ln_g[layer, 1], ln_b[layer, 1])
    return h
```

```python
import functools

import jax
import jax.numpy as jnp
from jax import lax
from jax.experimental import pallas as pl
from jax.experimental.pallas import tpu as pltpu

F32 = jnp.float32
BF16 = jnp.bfloat16

GLA_HEADS = 4
GATE_RANK = 16
GATE_TAU = 16.0
GLA_CHUNK = 64
SB_HEAD_DIM = 64
LN_EPS = 1e-5
RMS_EPS = 1e-6

LANES = 128
VMEM_LIMIT = 56 * 1024 * 1024

GLA_TOKENS = 256
MLP_TOKENS = 512
MLP_FF_CHUNK = 1024
PROJ_TOKENS = 512
SB_TQ = 128
SB_TK = 256


def _const_spec(shape):
    nd = len(shape)
    return pl.BlockSpec(shape, lambda *_: (0,) * nd, pipeline_mode=pl.Buffered(1))


def _layer_norm(y, g, b):
    mu = jnp.mean(y, axis=-1, keepdims=True)
    yc = y - mu
    var = jnp.mean(yc * yc, axis=-1, keepdims=True)
    return yc * lax.rsqrt(var + LN_EPS) * g + b


def _log_sigmoid(x):
    return jnp.minimum(x, 0.0) - jnp.log1p(jnp.exp(-jnp.abs(x)))


def _split3_bf16(x):
    hi = x.astype(BF16)
    r1 = x - hi.astype(F32)
    mid = r1.astype(BF16)
    lo = (r1 - mid.astype(F32)).astype(BF16)
    return hi, mid, lo


def _gla_kernel(h_ref, wqkvg_ref, wa1_ref, wa2_ref, ba_ref, ng_ref, wo_ref, lng_ref, lnb_ref,
                tri_ref, o_ref, state_ref, og_ref, *, alpha, dqk, dv):
    hk = dqk // GLA_HEADS
    hv = dv // GLA_HEADS
    ts = h_ref.shape[0]
    n_chunks = ts // GLA_CHUNK

    @pl.when(pl.program_id(1) == 0)
    def _():
        state_ref[...] = jnp.zeros_like(state_ref)

    h = h_ref[...]
    hb = h.astype(BF16)
    proj = jnp.dot(hb, wqkvg_ref[...], preferred_element_type=F32)
    a1 = jnp.dot(hb, wa1_ref[...], preferred_element_type=F32)
    gate = jnp.dot(a1.astype(BF16), wa2_ref[...], preferred_element_type=F32) + ba_ref[...]
    log_a = _log_sigmoid(gate) * (1.0 / GATE_TAU)

    tri3 = tri_ref[...]
    row = lax.broadcasted_iota(jnp.int32, (GLA_CHUNK, GLA_CHUNK), 0)
    col = lax.broadcasted_iota(jnp.int32, (GLA_CHUNK, GLA_CHUNK), 1)
    causal = row >= col
    q_scale = hk ** -0.5

    for c in range(n_chunks):
        r0 = c * GLA_CHUNK
        la = log_a[r0:r0 + GLA_CHUNK, :]
        hi, mid, lo = _split3_bf16(la)
        b_all = jnp.dot(tri3, jnp.concatenate([hi, mid, lo], axis=0),
                        preferred_element_type=F32)
        for hd in range(GLA_HEADS):
            q = proj[r0:r0 + GLA_CHUNK, hd * hk:(hd + 1) * hk]
            k = proj[r0:r0 + GLA_CHUNK, dqk + hd * hk:dqk + (hd + 1) * hk]
            v = proj[r0:r0 + GLA_CHUNK, 2 * dqk + hd * hv:2 * dqk + (hd + 1) * hv]
            g = proj[r0:r0 + GLA_CHUNK, 2 * dqk + dv + hd * hv:2 * dqk + dv + (hd + 1) * hv]
            b = b_all[:, hd * hk:(hd + 1) * hk]
            b_last = b[GLA_CHUNK - 1:GLA_CHUNK, :]
            q_dec = ((q * q_scale) * jnp.exp(b)).astype(BF16)
            k_dec = (k * jnp.exp(-b)).astype(BF16)
            k_end = (k * jnp.exp(b_last - b)).astype(BF16)
            vb = v.astype(BF16)
            att = lax.dot_general(q_dec, k_dec, (((1,), (1,)), ((), ())),
                                  preferred_element_type=F32)
            att = jnp.where(causal, att, 0.0).astype(BF16)
            st = state_ref[hd]
            o = jnp.dot(att, vb, preferred_element_type=F32)
            o = o + lax.dot_general(q_dec, st.astype(BF16), (((1,), (1,)), ((), ())),
                                    preferred_element_type=F32)
            ut = lax.dot_general(vb, k_end, (((0,), (0,)), ((), ())),
                                 preferred_element_type=F32)
            state_ref[hd] = st * jnp.exp(b_last) + ut
            o = o * lax.rsqrt(jnp.mean(o * o, axis=-1, keepdims=True) + RMS_EPS)
            o = o * ng_ref[...]
            o = o * (g / (1.0 + jnp.exp(-g)))
            og_ref[r0:r0 + GLA_CHUNK, hd * hv:(hd + 1) * hv] = o.astype(BF16)

    mix = jnp.dot(og_ref[...], wo_ref[...], preferred_element_type=F32)
    o_ref[...] = _layer_norm(alpha * h + mix, lng_ref[...], lnb_ref[...])


def _gla_layer(h, w_in, w_a2, b_a, norm_g, w_o, ln_g, ln_b, alpha):
    bsz, seq, d = h.shape
    dqk = w_a2.shape[1]
    dv = w_o.shape[0]
    n_main = 2 * dqk + 2 * dv
    hv = dv // GLA_HEADS
    hk = dqk // GLA_HEADS
    wqkvg = w_in[:, :n_main].astype(BF16)
    wa1 = jnp.pad(w_in[:, n_main:], ((0, 0), (0, LANES - GATE_RANK))).astype(BF16)
    wa2 = jnp.pad(w_a2, ((0, LANES - GATE_RANK), (0, 0))).astype(BF16)
    tri = (jnp.arange(GLA_CHUNK)[:, None] >= jnp.arange(GLA_CHUNK)[None, :]).astype(BF16)
    tri3 = jnp.concatenate([tri, tri, tri], axis=1)
    ts = GLA_TOKENS
    kern = functools.partial(_gla_kernel, alpha=alpha, dqk=dqk, dv=dv)
    return pl.pallas_call(
        kern,
        grid=(bsz, seq // ts),
        in_specs=[
            pl.BlockSpec((None, ts, d), lambda b, s: (b, s, 0)),
            _const_spec((d, n_main)),
            _const_spec((d, LANES)),
            _const_spec((LANES, dqk)),
            _const_spec((1, dqk)),
            _const_spec((1, hv)),
            _const_spec((dv, d)),
            _const_spec((1, d)),
            _const_spec((1, d)),
            _const_spec((GLA_CHUNK, 3 * GLA_CHUNK)),
        ],
        out_specs=pl.BlockSpec((None, ts, d), lambda b, s: (b, s, 0)),
        out_shape=jax.ShapeDtypeStruct((bsz, seq, d), F32),
        scratch_shapes=[
            pltpu.VMEM((GLA_HEADS, hv, hk), F32),
            pltpu.VMEM((ts, dv), BF16),
        ],
        compiler_params=pltpu.CompilerParams(
            dimension_semantics=("arbitrary", "arbitrary"), vmem_limit_bytes=VMEM_LIMIT),
        name="gla_layer",
    )(h, wqkvg, wa1, wa2, b_a.reshape(1, dqk), norm_g.reshape(1, hv), w_o.astype(BF16),
      ln_g.reshape(1, d), ln_b.reshape(1, d), tri3)


def _mlp_kernel(h_ref, wup_ref, wdn_ref, g_ref, b_ref, o_ref, *, alpha):
    h = h_ref[...]
    hb = h.astype(BF16)
    dff = wup_ref.shape[1]
    acc = None
    for f in range(0, dff, MLP_FF_CHUNK):
        u = jnp.dot(hb, wup_ref[:, f:f + MLP_FF_CHUNK], preferred_element_type=F32)
        u = jnp.maximum(u, 0.0)
        u = (u * u).astype(BF16)
        dn = jnp.dot(u, wdn_ref[f:f + MLP_FF_CHUNK, :], preferred_element_type=F32)
        acc = dn if acc is None else acc + dn
    o_ref[...] = _layer_norm(alpha * h + acc, g_ref[...], b_ref[...])


def _mlp_layer(h2, w_up, w_down, ln_g, ln_b, alpha):
    t, d = h2.shape
    dff = w_up.shape[1]
    tm = MLP_TOKENS
    return pl.pallas_call(
        functools.partial(_mlp_kernel, alpha=alpha),
        grid=(t // tm,),
        in_specs=[
            pl.BlockSpec((tm, d), lambda i: (i, 0)),
            _const_spec((d, dff)),
            _const_spec((dff, d)),
            _const_spec((1, d)),
            _const_spec((1, d)),
        ],
        out_specs=pl.BlockSpec((tm, d), lambda i: (i, 0)),
        out_shape=jax.ShapeDtypeStruct((t, d), F32),
        compiler_params=pltpu.CompilerParams(
            dimension_semantics=("arbitrary",), vmem_limit_bytes=VMEM_LIMIT),
        name="mlp_layer",
    )(h2, w_up.astype(BF16), w_down.astype(BF16), ln_g.reshape(1, d), ln_b.reshape(1, d))


def _sb_proj_kernel(h_ref, w_ref, *out_refs):
    hb = h_ref[...].astype(BF16)
    d = h_ref.shape[1]
    res = jnp.dot(hb, w_ref[...], preferred_element_type=F32)
    for n, o_ref in enumerate(out_refs):
        for hp in range(d // LANES):
            c0 = n * d + hp * LANES
            o_ref[hp] = res[:, c0:c0 + LANES].astype(BF16)


def _sb_proj(h, w):
    bsz, seq, d = h.shape
    n_out = w.shape[1] // d
    tm = PROJ_TOKENS
    hp = d // LANES
    out_sds = jax.ShapeDtypeStruct((bsz, hp, seq, LANES), BF16)
    out_spec = pl.BlockSpec((None, hp, tm, LANES), lambda b, s: (b, 0, s, 0))
    return pl.pallas_call(
        _sb_proj_kernel,
        grid=(bsz, seq // tm),
        in_specs=[
            pl.BlockSpec((None, tm, d), lambda b, s: (b, s, 0)),
            _const_spec((d, n_out * d)),
        ],
        out_specs=[out_spec] * n_out,
        out_shape=[out_sds] * n_out,
        compiler_params=pltpu.CompilerParams(
            dimension_semantics=("arbitrary", "arbitrary"), vmem_limit_bytes=VMEM_LIMIT),
        name="sb_proj",
    )(h, w)


def _sb_tile(qs, k_blk, v_blk, tt, carry, q0, k0, masked):
    acc, r = carry
    z = lax.dot_general(qs, k_blk, (((1,), (1,)), ((), ())), preferred_element_type=F32)
    sp = jnp.log1p(jnp.exp(-jnp.abs(z)))
    l1m = -(jnp.maximum(z, 0.0) + sp)
    lsz = l1m + z
    if masked:
        rows = lax.broadcasted_iota(jnp.int32, z.shape, 0)
        cols = lax.broadcasted_iota(jnp.int32, z.shape, 1)
        t_idx = q0 + jnp.where(rows >= SB_TQ, rows - SB_TQ, rows)
        valid = (k0 + cols) < t_idx
        l1m = jnp.where(valid, l1m, 0.0)
    hi = l1m.astype(BF16)
    lo = (l1m - hi.astype(F32)).astype(BF16)
    cs = jnp.dot(jnp.concatenate([hi, lo], axis=1), tt, preferred_element_type=F32)
    w = jnp.exp(lsz + cs + r)
    if masked:
        w = jnp.where(valid, w, 0.0)
    wb = w.astype(BF16)
    lane = lax.broadcasted_iota(jnp.int32, v_blk.shape, 1)
    v0 = jnp.where(lane < SB_HEAD_DIM, v_blk, jnp.zeros_like(v_blk))
    v1 = jnp.where(lane >= SB_HEAD_DIM, v_blk, jnp.zeros_like(v_blk))
    w2 = jnp.concatenate([wb[:SB_TQ], wb[SB_TQ:]], axis=1)
    v2 = jnp.concatenate([v0, v1], axis=0)
    acc = acc + jnp.dot(w2, v2, preferred_element_type=F32)
    r = r + jnp.sum(l1m, axis=-1, keepdims=True)
    return acc, r


def _sb_attn_kernel(q_ref, k_ref, v_ref, tt_ref, o_ref):
    seq = q_ref.shape[0]
    tt = tt_ref[...]
    lane_q = lax.broadcasted_iota(jnp.int32, (SB_TQ, LANES), 1)

    def q_block(i, _):
        q0 = pl.multiple_of(i * SB_TQ, SB_TQ)
        q = q_ref[pl.ds(q0, SB_TQ), :]
        zq = jnp.zeros_like(q)
        qs = jnp.concatenate([jnp.where(lane_q < SB_HEAD_DIM, q, zq),
                              jnp.where(lane_q >= SB_HEAD_DIM, q, zq)], axis=0)
        jd = (i * SB_TQ) // SB_TK

        def tile(jb, carry, masked):
            k0 = pl.multiple_of(jb * SB_TK, SB_TK)
            return _sb_tile(qs, k_ref[pl.ds(k0, SB_TK), :], v_ref[pl.ds(k0, SB_TK), :], tt,
                            carry, q0, k0, masked)

        carry = (jnp.zeros((SB_TQ, LANES), F32), jnp.zeros((2 * SB_TQ, 1), F32))
        carry = tile(jd, carry, True)
        acc, _ = lax.fori_loop(0, jd, lambda it, cr: tile(jd - 1 - it, cr, False), carry)
        o_ref[pl.ds(q0, SB_TQ), :] = acc.astype(o_ref.dtype)
        return 0

    lax.fori_loop(0, seq // SB_TQ, q_block, 0)


def _sb_attention(q, k, v):
    bsz, hp, seq, _ = q.shape
    ar = jnp.arange(SB_TK)
    upper = (ar[:, None] > ar[None, :]).astype(BF16)
    tt = jnp.concatenate([upper, upper], axis=0)
    slab = pl.BlockSpec((None, None, seq, LANES), lambda b, p: (b, p, 0, 0))
    return pl.pallas_call(
        _sb_attn_kernel,
        grid=(bsz, hp),
        in_specs=[slab, slab, slab, _const_spec((2 * SB_TK, SB_TK))],
        out_specs=pl.BlockSpec((None, seq, LANES), lambda b, p: (b, 0, p)),
        out_shape=jax.ShapeDtypeStruct((bsz, seq, hp * LANES), BF16),
        compiler_params=pltpu.CompilerParams(
            dimension_semantics=("arbitrary", "arbitrary"), vmem_limit_bytes=VMEM_LIMIT),
        name="sb_attention",
    )(q, k, v, tt)


def _sb_out_kernel(h_ref, a_ref, wo_ref, g_ref, b_ref, o_ref, *, alpha):
    mix = jnp.dot(a_ref[...], wo_ref[...], preferred_element_type=F32)
    o_ref[...] = _layer_norm(alpha * h_ref[...] + mix, g_ref[...], b_ref[...])


def _sb_out(h2, a2, w_o, ln_g, ln_b, alpha):
    t, d = h2.shape
    tm = PROJ_TOKENS
    return pl.pallas_call(
        functools.partial(_sb_out_kernel, alpha=alpha),
        grid=(t // tm,),
        in_specs=[
            pl.BlockSpec((tm, d), lambda i: (i, 0)),
            pl.BlockSpec((tm, d), lambda i: (i, 0)),
            _const_spec((d, d)),
            _const_spec((1, d)),
            _const_spec((1, d)),
        ],
        out_specs=pl.BlockSpec((tm, d), lambda i: (i, 0)),
        out_shape=jax.ShapeDtypeStruct((t, d), F32),
        compiler_params=pltpu.CompilerParams(
            dimension_semantics=("arbitrary",), vmem_limit_bytes=VMEM_LIMIT),
        name="sb_out",
    )(h2, a2, w_o.astype(BF16), ln_g.reshape(1, d), ln_b.reshape(1, d))


def kernel(x, gla_w_in, gla_w_a2, gla_b_a, gla_norm_g, gla_w_o, kv_w, sb_w_q, sb_w_o,
           mlp_w_up, mlp_w_down, ln_g, ln_b):
    bsz, seq, d = x.shape
    depth = mlp_w_up.shape[0]
    n_a = gla_w_in.shape[0]
    alpha = (2.0 * depth) ** 0.25
    q_scale = SB_HEAD_DIM ** -0.5

    h = x
    k_sh = v_sh = None
    for layer in range(depth):
        if layer < n_a:
            h = _gla_layer(h, gla_w_in[layer], gla_w_a2[layer], gla_b_a[layer], gla_norm_g[layer],
                           gla_w_o[layer], ln_g[layer, 0], ln_b[layer, 0], alpha)
        else:
            j = layer - n_a
            wq = (sb_w_q[j] * q_scale).astype(BF16)
            if j == 0:
                q, k_sh, v_sh = _sb_proj(h, jnp.concatenate([wq, kv_w.astype(BF16)], axis=1))
            else:
                (q,) = _sb_proj(h, wq)
            att = _sb_attention(q, k_sh, v_sh)
            h = _sb_out(h.reshape(bsz * seq, d), att.reshape(bsz * seq, d), sb_w_o[j],
                        ln_g[layer, 0], ln_b[layer, 0], alpha).reshape(bsz, seq, d)
        h = _mlp_layer(h.reshape(bsz * seq, d), mlp_w_up[layer], mlp_w_down[layer],
                       ln_g[layer, 1], ln_b[layer, 1], alpha).reshape(bsz, seq, d)
    return h
```

```python
import functools

import jax
import jax.numpy as jnp
from jax import lax
from jax.experimental import pallas as pl
from jax.experimental.pallas import tpu as pltpu

F32 = jnp.float32
BF16 = jnp.bfloat16

GLA_HEADS = 4
GATE_RANK = 16
GATE_TAU = 16.0
GLA_CHUNK = 64
SB_HEAD_DIM = 64
LN_EPS = 1e-5
RMS_EPS = 1e-6

LANES = 128
VMEM_LIMIT = 56 * 1024 * 1024

GLA_TOKENS = 256
MLP_TOKENS = 512
MLP_FF_CHUNK = 1024
PROJ_TOKENS = 512
SB_T = 256
SB_MASK_VALUE = -1e30
SB_LOG_WEIGHT_FLOOR = 88.0


def _const_spec(shape):
    nd = len(shape)
    return pl.BlockSpec(shape, lambda *_: (0,) * nd, pipeline_mode=pl.Buffered(1))


def _layer_norm(y, g, b):
    mu = jnp.mean(y, axis=-1, keepdims=True)
    yc = y - mu
    var = jnp.mean(yc * yc, axis=-1, keepdims=True)
    return yc * lax.rsqrt(var + LN_EPS) * g + b


def _log_sigmoid(x):
    return jnp.minimum(x, 0.0) - jnp.log1p(jnp.exp(-jnp.abs(x)))


def _split3_bf16(x):
    hi = x.astype(BF16)
    r1 = x - hi.astype(F32)
    mid = r1.astype(BF16)
    lo = (r1 - mid.astype(F32)).astype(BF16)
    return hi, mid, lo


def _gla_kernel(h_ref, wqkvg_ref, wa1_ref, wa2_ref, ba_ref, ng_ref, wo_ref, lng_ref, lnb_ref,
                tri_ref, o_ref, state_ref, og_ref, *, alpha, dqk, dv):
    hk = dqk // GLA_HEADS
    hv = dv // GLA_HEADS
    ts = h_ref.shape[0]
    n_chunks = ts // GLA_CHUNK

    @pl.when(pl.program_id(1) == 0)
    def _():
        state_ref[...] = jnp.zeros_like(state_ref)

    h = h_ref[...]
    hb = h.astype(BF16)
    proj = jnp.dot(hb, wqkvg_ref[...], preferred_element_type=F32)
    a1 = jnp.dot(hb, wa1_ref[...], preferred_element_type=F32)
    gate = jnp.dot(a1.astype(BF16), wa2_ref[...], preferred_element_type=F32) + ba_ref[...]
    log_a = _log_sigmoid(gate) * (1.0 / GATE_TAU)

    tri3 = tri_ref[...]
    row = lax.broadcasted_iota(jnp.int32, (GLA_CHUNK, GLA_CHUNK), 0)
    col = lax.broadcasted_iota(jnp.int32, (GLA_CHUNK, GLA_CHUNK), 1)
    causal = row >= col
    q_scale = hk ** -0.5

    for c in range(n_chunks):
        r0 = c * GLA_CHUNK
        la = log_a[r0:r0 + GLA_CHUNK, :]
        hi, mid, lo = _split3_bf16(la)
        b_all = jnp.dot(tri3, jnp.concatenate([hi, mid, lo], axis=0),
                        preferred_element_type=F32)
        for hd in range(GLA_HEADS):
            q = proj[r0:r0 + GLA_CHUNK, hd * hk:(hd + 1) * hk]
            k = proj[r0:r0 + GLA_CHUNK, dqk + hd * hk:dqk + (hd + 1) * hk]
            v = proj[r0:r0 + GLA_CHUNK, 2 * dqk + hd * hv:2 * dqk + (hd + 1) * hv]
            g = proj[r0:r0 + GLA_CHUNK, 2 * dqk + dv + hd * hv:2 * dqk + dv + (hd + 1) * hv]
            b = b_all[:, hd * hk:(hd + 1) * hk]
            b_last = b[GLA_CHUNK - 1:GLA_CHUNK, :]
            q_dec = ((q * q_scale) * jnp.exp(b)).astype(BF16)
            k_dec = (k * jnp.exp(-b)).astype(BF16)
            k_end = (k * jnp.exp(b_last - b)).astype(BF16)
            vb = v.astype(BF16)
            att = lax.dot_general(q_dec, k_dec, (((1,), (1,)), ((), ())),
                                  preferred_element_type=F32)
            att = jnp.where(causal, att, 0.0).astype(BF16)
            st = state_ref[hd]
            o = jnp.dot(att, vb, preferred_element_type=F32)
            o = o + lax.dot_general(q_dec, st.astype(BF16), (((1,), (1,)), ((), ())),
                                    preferred_element_type=F32)
            ut = lax.dot_general(vb, k_end, (((0,), (0,)), ((), ())),
                                 preferred_element_type=F32)
            state_ref[hd] = st * jnp.exp(b_last) + ut
            o = o * lax.rsqrt(jnp.mean(o * o, axis=-1, keepdims=True) + RMS_EPS)
            o = o * ng_ref[...]
            o = o * (g / (1.0 + jnp.exp(-g)))
            og_ref[r0:r0 + GLA_CHUNK, hd * hv:(hd + 1) * hv] = o.astype(BF16)

    mix = jnp.dot(og_ref[...], wo_ref[...], preferred_element_type=F32)
    o_ref[...] = _layer_norm(alpha * h + mix, lng_ref[...], lnb_ref[...])


def _gla_layer(h, w_in, w_a2, b_a, norm_g, w_o, ln_g, ln_b, alpha):
    bsz, seq, d = h.shape
    dqk = w_a2.shape[1]
    dv = w_o.shape[0]
    n_main = 2 * dqk + 2 * dv
    hv = dv // GLA_HEADS
    hk = dqk // GLA_HEADS
    wqkvg = w_in[:, :n_main].astype(BF16)
    wa1 = jnp.pad(w_in[:, n_main:], ((0, 0), (0, LANES - GATE_RANK))).astype(BF16)
    wa2 = jnp.pad(w_a2, ((0, LANES - GATE_RANK), (0, 0))).astype(BF16)
    tri = (jnp.arange(GLA_CHUNK)[:, None] >= jnp.arange(GLA_CHUNK)[None, :]).astype(BF16)
    tri3 = jnp.concatenate([tri, tri, tri], axis=1)
    ts = GLA_TOKENS
    kern = functools.partial(_gla_kernel, alpha=alpha, dqk=dqk, dv=dv)
    return pl.pallas_call(
        kern,
        grid=(bsz, seq // ts),
        in_specs=[
            pl.BlockSpec((None, ts, d), lambda b, s: (b, s, 0)),
            _const_spec((d, n_main)),
            _const_spec((d, LANES)),
            _const_spec((LANES, dqk)),
            _const_spec((1, dqk)),
            _const_spec((1, hv)),
            _const_spec((dv, d)),
            _const_spec((1, d)),
            _const_spec((1, d)),
            _const_spec((GLA_CHUNK, 3 * GLA_CHUNK)),
        ],
        out_specs=pl.BlockSpec((None, ts, d), lambda b, s: (b, s, 0)),
        out_shape=jax.ShapeDtypeStruct((bsz, seq, d), F32),
        scratch_shapes=[
            pltpu.VMEM((GLA_HEADS, hv, hk), F32),
            pltpu.VMEM((ts, dv), BF16),
        ],
        compiler_params=pltpu.CompilerParams(
            dimension_semantics=("arbitrary", "arbitrary"), vmem_limit_bytes=VMEM_LIMIT),
        name="gla_layer",
    )(h, wqkvg, wa1, wa2, b_a.reshape(1, dqk), norm_g.reshape(1, hv), w_o.astype(BF16),
      ln_g.reshape(1, d), ln_b.reshape(1, d), tri3)


def _mlp_kernel(h_ref, wup_ref, wdn_ref, g_ref, b_ref, o_ref, *, alpha):
    h = h_ref[...]
    hb = h.astype(BF16)
    dff = wup_ref.shape[1]
    acc = None
    for f in range(0, dff, MLP_FF_CHUNK):
        u = jnp.dot(hb, wup_ref[:, f:f + MLP_FF_CHUNK], preferred_element_type=F32)
        u = jnp.maximum(u, 0.0)
        u = (u * u).astype(BF16)
        dn = jnp.dot(u, wdn_ref[f:f + MLP_FF_CHUNK, :], preferred_element_type=F32)
        acc = dn if acc is None else acc + dn
    o_ref[...] = _layer_norm(alpha * h + acc, g_ref[...], b_ref[...])


def _mlp_layer(h2, w_up, w_down, ln_g, ln_b, alpha):
    t, d = h2.shape
    dff = w_up.shape[1]
    tm = MLP_TOKENS
    return pl.pallas_call(
        functools.partial(_mlp_kernel, alpha=alpha),
        grid=(t // tm,),
        in_specs=[
            pl.BlockSpec((tm, d), lambda i: (i, 0)),
            _const_spec((d, dff)),
            _const_spec((dff, d)),
            _const_spec((1, d)),
            _const_spec((1, d)),
        ],
        out_specs=pl.BlockSpec((tm, d), lambda i: (i, 0)),
        out_shape=jax.ShapeDtypeStruct((t, d), F32),
        compiler_params=pltpu.CompilerParams(
            dimension_semantics=("arbitrary",), vmem_limit_bytes=VMEM_LIMIT),
        name="mlp_layer",
    )(h2, w_up.astype(BF16), w_down.astype(BF16), ln_g.reshape(1, d), ln_b.reshape(1, d))


def _sb_proj_kernel(h_ref, w_ref, *out_refs):
    hb = h_ref[...].astype(BF16)
    d = h_ref.shape[1]
    res = jnp.dot(hb, w_ref[...], preferred_element_type=F32)
    for n, o_ref in enumerate(out_refs):
        for hp in range(d // LANES):
            c0 = n * d + hp * LANES
            o_ref[hp] = res[:, c0:c0 + LANES].astype(BF16)


def _sb_proj(h, w):
    bsz, seq, d = h.shape
    n_out = w.shape[1] // d
    tm = PROJ_TOKENS
    hp = d // LANES
    out_sds = jax.ShapeDtypeStruct((bsz, hp, seq, LANES), BF16)
    out_spec = pl.BlockSpec((None, hp, tm, LANES), lambda b, s: (b, 0, s, 0))
    return pl.pallas_call(
        _sb_proj_kernel,
        grid=(bsz, seq // tm),
        in_specs=[
            pl.BlockSpec((None, tm, d), lambda b, s: (b, s, 0)),
            _const_spec((d, n_out * d)),
        ],
        out_specs=[out_spec] * n_out,
        out_shape=[out_sds] * n_out,
        compiler_params=pltpu.CompilerParams(
            dimension_semantics=("arbitrary", "arbitrary"), vmem_limit_bytes=VMEM_LIMIT),
        name="sb_proj",
    )(h, w)


def _sb_tile(qs, k_blk, v_blk, tt_ref, bias, r):
    z = lax.dot_general(qs, k_blk, (((1,), (1,)), ((), ())), preferred_element_type=F32)
    if bias is not None:
        z = z + bias
    t = jnp.maximum(z, 0.0) + jnp.log(1.0 + jnp.exp(-jnp.abs(z)))
    hi = t.astype(BF16)
    lo = (t - hi.astype(F32)).astype(BF16)
    cs = jnp.dot(jnp.concatenate([hi, lo], axis=1), tt_ref[...], preferred_element_type=F32)
    logw = (z - t) - cs
    if r is not None:
        logw = logw - r
    w = jnp.exp(logw).astype(BF16)
    lane = lax.broadcasted_iota(jnp.int32, v_blk.shape, 1)
    zv = jnp.zeros_like(v_blk)
    v2 = jnp.concatenate([jnp.where(lane < SB_HEAD_DIM, v_blk, zv),
                          jnp.where(lane >= SB_HEAD_DIM, v_blk, zv)], axis=0)
    w2 = jnp.concatenate([w[:SB_T], w[SB_T:]], axis=1)
    pv = jnp.dot(w2, v2, preferred_element_type=F32)
    return pv, jnp.sum(t, axis=-1, keepdims=True)


def _sb_attn_kernel(q_ref, k_ref, v_ref, tt_ref, bias_ref, o_ref):
    seq = q_ref.shape[0]
    lane_q = lax.broadcasted_iota(jnp.int32, (SB_T, LANES), 1)

    def block(m, has_prev):
        q0 = pl.multiple_of(m * SB_T, SB_T)
        q = q_ref[pl.ds(q0, SB_T), :]
        zq = jnp.zeros_like(q)
        qs = jnp.concatenate([jnp.where(lane_q < SB_HEAD_DIM, q, zq),
                              jnp.where(lane_q >= SB_HEAD_DIM, q, zq)], axis=0)

        def tile(k0, bias, r):
            return _sb_tile(qs, k_ref[pl.ds(k0, SB_T), :], v_ref[pl.ds(k0, SB_T), :], tt_ref,
                            bias, r)

        acc, rsum = tile(q0, bias_ref[...], None)
        if has_prev:
            pv, rs = tile(pl.multiple_of(q0 - SB_T, SB_T), None, rsum)
            acc = acc + pv
            rsum = rsum + rs

            def more(c):
                j, _, _, rmin = c
                return jnp.logical_and(j >= 0, rmin < SB_LOG_WEIGHT_FLOOR)

            def step(c):
                j, acc, rsum, _ = c
                pv, rs = tile(pl.multiple_of(j * SB_T, SB_T), None, rsum)
                rsum = rsum + rs
                return j - 1, acc + pv, rsum, jnp.min(rsum)

            _, acc, _, _ = lax.while_loop(more, step, (m - 2, acc, rsum, jnp.min(rsum)))
        o_ref[pl.ds(q0, SB_T), :] = acc.astype(o_ref.dtype)

    block(0, False)

    def body(m, carry):
        block(m, True)
        return carry

    lax.fori_loop(1, seq // SB_T, body, 0)


def _sb_attention(q, k, v):
    bsz, hp, seq, _ = q.shape
    ar = jnp.arange(SB_T)
    upper = (ar[:, None] > ar[None, :]).astype(BF16)
    tt = jnp.concatenate([upper, upper], axis=0)
    causal = ar[None, :] < ar[:, None]
    bias = jnp.where(jnp.concatenate([causal, causal], axis=0), 0.0, SB_MASK_VALUE).astype(F32)
    slab = pl.BlockSpec((None, None, seq, LANES), lambda b, p: (b, p, 0, 0))
    return pl.pallas_call(
        _sb_attn_kernel,
        grid=(bsz, hp),
        in_specs=[slab, slab, slab, _const_spec((2 * SB_T, SB_T)), _const_spec((2 * SB_T, SB_T))],
        out_specs=pl.BlockSpec((None, seq, LANES), lambda b, p: (b, 0, p)),
        out_shape=jax.ShapeDtypeStruct((bsz, seq, hp * LANES), BF16),
        compiler_params=pltpu.CompilerParams(
            dimension_semantics=("arbitrary", "arbitrary"), vmem_limit_bytes=VMEM_LIMIT),
        name="sb_attention",
    )(q, k, v, tt, bias)


def _sb_out_kernel(h_ref, a_ref, wo_ref, g_ref, b_ref, o_ref, *, alpha):
    mix = jnp.dot(a_ref[...], wo_ref[...], preferred_element_type=F32)
    o_ref[...] = _layer_norm(alpha * h_ref[...] + mix, g_ref[...], b_ref[...])


def _sb_out(h2, a2, w_o, ln_g, ln_b, alpha):
    t, d = h2.shape
    tm = PROJ_TOKENS
    return pl.pallas_call(
        functools.partial(_sb_out_kernel, alpha=alpha),
        grid=(t // tm,),
        in_specs=[
            pl.BlockSpec((tm, d), lambda i: (i, 0)),
            pl.BlockSpec((tm, d), lambda i: (i, 0)),
            _const_spec((d, d)),
            _const_spec((1, d)),
            _const_spec((1, d)),
        ],
        out_specs=pl.BlockSpec((tm, d), lambda i: (i, 0)),
        out_shape=jax.ShapeDtypeStruct((t, d), F32),
        compiler_params=pltpu.CompilerParams(
            dimension_semantics=("arbitrary",), vmem_limit_bytes=VMEM_LIMIT),
        name="sb_out",
    )(h2, a2, w_o.astype(BF16), ln_g.reshape(1, d), ln_b.reshape(1, d))


def kernel(x, gla_w_in, gla_w_a2, gla_b_a, gla_norm_g, gla_w_o, kv_w, sb_w_q, sb_w_o,
           mlp_w_up, mlp_w_down, ln_g, ln_b):
    bsz, seq, d = x.shape
    depth = mlp_w_up.shape[0]
    n_a = gla_w_in.shape[0]
    alpha = (2.0 * depth) ** 0.25
    q_scale = SB_HEAD_DIM ** -0.5

    h = x
    k_sh = v_sh = None
    for layer in range(depth):
        if layer < n_a:
            h = _gla_layer(h, gla_w_in[layer], gla_w_a2[layer], gla_b_a[layer], gla_norm_g[layer],
                           gla_w_o[layer], ln_g[layer, 0], ln_b[layer, 0], alpha)
        else:
            j = layer - n_a
            wq = (sb_w_q[j] * q_scale).astype(BF16)
            if j == 0:
                q, k_sh, v_sh = _sb_proj(h, jnp.concatenate([wq, kv_w.astype(BF16)], axis=1))
            else:
                (q,) = _sb_proj(h, wq)
            att = _sb_attention(q, k_sh, v_sh)
            h = _sb_out(h.reshape(bsz * seq, d), att.reshape(bsz * seq, d), sb_w_o[j],
                        ln_g[layer, 0], ln_b[layer, 0], alpha).reshape(bsz, seq, d)
        h = _mlp_layer(h.reshape(bsz * seq, d), mlp_w_up[layer], mlp_w_down[layer],
                       ln_g[layer, 1], ln_b[layer, 1], alpha).reshape(bsz, seq, d)
    return h
```

```python
import functools

import jax
import jax.numpy as jnp
from jax import lax
from jax.experimental import pallas as pl
from jax.experimental.pallas import tpu as pltpu

F32 = jnp.float32
BF16 = jnp.bfloat16

GLA_HEADS = 4
GATE_RANK = 16
GATE_TAU = 16.0
GLA_CHUNK = 64
SB_HEAD_DIM = 64
LN_EPS = 1e-5
RMS_EPS = 1e-6

LANES = 128
VMEM_LIMIT = 56 * 1024 * 1024

GLA_TOKENS = 256
GLA_PROJ_PIECE = 256
MLP_TOKENS = 512
MLP_FF_CHUNK = 1024
PROJ_TOKENS = 512
SB_T = 256
SB_SUM_ROWS = 16
SB_EXP_CLAMP = 88.0
SB_MASK_VALUE = -1e30
SB_LOG_WEIGHT_FLOOR = 88.0


def _const_spec(shape):
    nd = len(shape)
    return pl.BlockSpec(shape, lambda *_: (0,) * nd, pipeline_mode=pl.Buffered(1))


def _layer_norm(y, g, b):
    mu = jnp.mean(y, axis=-1, keepdims=True)
    yc = y - mu
    var = jnp.mean(yc * yc, axis=-1, keepdims=True)
    return yc * lax.rsqrt(var + LN_EPS) * g + b


def _log_sigmoid(x):
    return jnp.minimum(x, 0.0) - jnp.log(1.0 + jnp.exp(-jnp.abs(x)))


def _split3_bf16(x):
    hi = x.astype(BF16)
    r1 = x - hi.astype(F32)
    mid = r1.astype(BF16)
    lo = (r1 - mid.astype(F32)).astype(BF16)
    return hi, mid, lo


def _gla_kernel(h_ref, wqkvg_ref, wa1_ref, wa2_ref, ba_ref, ng_ref, wo_ref, lng_ref, lnb_ref,
                tri_ref, o_ref, state_ref, og_ref, proj0_ref, proj1_ref, a10_ref, a11_ref,
                hres0_ref, hres1_ref, hb_ref, *, alpha, dqk, dv, tiles_per_seq):
    g = pl.program_id(0)
    stash = ((proj0_ref, a10_ref, hres0_ref), (proj1_ref, a11_ref, hres1_ref))

    @pl.when(g == 0)
    def _():
        for ref in stash[1]:
            ref[...] = jnp.zeros_like(ref)

    @pl.when(lax.rem(g + tiles_per_seq - 1, tiles_per_seq) == 0)
    def _():
        state_ref[...] = jnp.zeros_like(state_ref)

    def step(cur, prev):
        pieces = _gla_projection_pieces(h_ref, wqkvg_ref, wa1_ref, hb_ref, *stash[cur])
        _gla_mix(*stash[prev], wa2_ref, ba_ref, ng_ref, wo_ref, lng_ref, lnb_ref, tri_ref,
                 o_ref, state_ref, og_ref, alpha=alpha, dqk=dqk, dv=dv,
                 between_units=lambda: next(pieces, None))
        for _ in pieces:
            pass

    @pl.when(lax.rem(g, 2) == 0)
    def _():
        step(0, 1)

    @pl.when(lax.rem(g, 2) == 1)
    def _():
        step(1, 0)


def _gla_projection_pieces(h_ref, wqkvg_ref, wa1_ref, hb_ref, proj_ref, a1_ref, hres_ref):
    h = h_ref[...]
    hres_ref[...] = h
    hb_ref[...] = h.astype(BF16)
    yield
    a1_ref[...] = jnp.dot(hb_ref[...], wa1_ref[...], preferred_element_type=F32)
    yield
    for c0 in range(0, wqkvg_ref.shape[1], GLA_PROJ_PIECE):
        cols = slice(c0, c0 + GLA_PROJ_PIECE)
        proj_ref[:, cols] = jnp.dot(hb_ref[...], wqkvg_ref[:, cols], preferred_element_type=F32)
        yield


def _gla_mix(proj_ref, a1_ref, hres_ref, wa2_ref, ba_ref, ng_ref, wo_ref, lng_ref, lnb_ref,
             tri_ref, o_ref, state_ref, og_ref, *, alpha, dqk, dv, between_units):
    hk = dqk // GLA_HEADS
    hv = dv // GLA_HEADS
    n_chunks = proj_ref.shape[0] // GLA_CHUNK

    for _ in range(4):
        between_units()
    gate = jnp.dot(a1_ref[...].astype(BF16), wa2_ref[...], preferred_element_type=F32) + ba_ref[...]
    log_a = _log_sigmoid(gate) * (1.0 / GATE_TAU)

    tri3 = tri_ref[...]
    row = lax.broadcasted_iota(jnp.int32, (GLA_CHUNK, GLA_CHUNK), 0)
    col = lax.broadcasted_iota(jnp.int32, (GLA_CHUNK, GLA_CHUNK), 1)
    causal = row >= col
    q_scale = hk ** -0.5

    b_chunks = []
    for c in range(n_chunks):
        hi, mid, lo = _split3_bf16(log_a[c * GLA_CHUNK:(c + 1) * GLA_CHUNK, :])
        b_chunks.append(jnp.dot(tri3, jnp.concatenate([hi, mid, lo], axis=0),
                                preferred_element_type=F32))

    def scores(c, hd):
        rows = slice(c * GLA_CHUNK, (c + 1) * GLA_CHUNK)
        q = proj_ref[rows, hd * hk:(hd + 1) * hk]
        k = proj_ref[rows, dqk + hd * hk:dqk + (hd + 1) * hk]
        b = b_chunks[c][:, hd * hk:(hd + 1) * hk]
        b_last = b[GLA_CHUNK - 1:GLA_CHUNK, :]
        q_dec = ((q * q_scale) * jnp.exp(b)).astype(BF16)
        k_dec = (k * jnp.exp(-b)).astype(BF16)
        k_end = (k * jnp.exp(b_last - b)).astype(BF16)
        att = lax.dot_general(q_dec, k_dec, (((1,), (1,)), ((), ())),
                              preferred_element_type=F32)
        return q_dec, k_end, b_last, att

    def finish(c, hd, q_dec, k_end, b_last, att):
        rows = slice(c * GLA_CHUNK, (c + 1) * GLA_CHUNK)
        v = proj_ref[rows, 2 * dqk + hd * hv:2 * dqk + (hd + 1) * hv]
        g = proj_ref[rows, 2 * dqk + dv + hd * hv:2 * dqk + dv + (hd + 1) * hv]
        vb = v.astype(BF16)
        att = jnp.where(causal, att, 0.0).astype(BF16)
        st = state_ref[hd]
        o = jnp.dot(jnp.concatenate([q_dec, att], axis=1),
                    jnp.concatenate([st.astype(BF16), vb], axis=0),
                    preferred_element_type=F32)
        u = lax.dot_general(k_end, vb, (((0,), (0,)), ((), ())),
                            preferred_element_type=F32)
        decay = jnp.exp(jnp.broadcast_to(b_last, (hk, hk)).T)
        state_ref[hd] = st * jnp.concatenate([decay] * (hv // hk), axis=1) + u
        o = o * lax.rsqrt(jnp.mean(o * o, axis=-1, keepdims=True) + RMS_EPS)
        o = o * ng_ref[...]
        o = o * (g / (1.0 + jnp.exp(-g)))
        og_ref[rows, hd * hv:(hd + 1) * hv] = o.astype(BF16)

    units = [(c, hd) for c in range(n_chunks) for hd in range(GLA_HEADS)]
    pending = scores(*units[0])
    for i, unit in enumerate(units):
        ahead = scores(*units[i + 1]) if i + 1 < len(units) else None
        if i % (len(units) // 8) == 0:
            between_units()
        finish(*unit, *pending)
        pending = ahead

    mix = jnp.dot(og_ref[...], wo_ref[...], preferred_element_type=F32)
    for _ in range(3):
        between_units()
    o_ref[...] = _layer_norm(alpha * hres_ref[...] + mix, lng_ref[...], lnb_ref[...])


def _gla_layer(h, w_in, w_a2, b_a, norm_g, w_o, ln_g, ln_b, alpha):
    bsz, seq, d = h.shape
    dqk = w_a2.shape[1]
    dv = w_o.shape[0]
    n_main = 2 * dqk + 2 * dv
    hv = dv // GLA_HEADS
    hk = dqk // GLA_HEADS
    wqkvg = w_in[:, :n_main].astype(BF16)
    wa1 = jnp.pad(w_in[:, n_main:], ((0, 0), (0, LANES - GATE_RANK))).astype(BF16)
    wa2 = jnp.pad(w_a2, ((0, LANES - GATE_RANK), (0, 0))).astype(BF16)
    tri = (jnp.arange(GLA_CHUNK)[:, None] >= jnp.arange(GLA_CHUNK)[None, :]).astype(BF16)
    tri3 = jnp.concatenate([tri, tri, tri], axis=1)
    ts = GLA_TOKENS
    n_tiles = bsz * seq // ts
    kern = functools.partial(_gla_kernel, alpha=alpha, dqk=dqk, dv=dv, tiles_per_seq=seq // ts)
    out = pl.pallas_call(
        kern,
        grid=(n_tiles + 1,),
        in_specs=[
            pl.BlockSpec((ts, d), lambda g: (jnp.minimum(g, n_tiles - 1), 0)),
            _const_spec((d, n_main)),
            _const_spec((d, LANES)),
            _const_spec((LANES, dqk)),
            _const_spec((1, dqk)),
            _const_spec((1, hv)),
            _const_spec((dv, d)),
            _const_spec((1, d)),
            _const_spec((1, d)),
            _const_spec((GLA_CHUNK, 3 * GLA_CHUNK)),
        ],
        out_specs=pl.BlockSpec((ts, d), lambda g: (jnp.maximum(g - 1, 0), 0)),
        out_shape=jax.ShapeDtypeStruct((bsz * seq, d), F32),
        scratch_shapes=[
            pltpu.VMEM((GLA_HEADS, hk, hv), F32),
            pltpu.VMEM((ts, dv), BF16),
            pltpu.VMEM((ts, n_main), F32), pltpu.VMEM((ts, n_main), F32),
            pltpu.VMEM((ts, LANES), F32), pltpu.VMEM((ts, LANES), F32),
            pltpu.VMEM((ts, d), F32), pltpu.VMEM((ts, d), F32),
            pltpu.VMEM((ts, d), BF16),
        ],
        compiler_params=pltpu.CompilerParams(
            dimension_semantics=("arbitrary",), vmem_limit_bytes=VMEM_LIMIT),
        name="gla_layer",
    )(h.reshape(bsz * seq, d), wqkvg, wa1, wa2, b_a.reshape(1, dqk), norm_g.reshape(1, hv),
      w_o.astype(BF16), ln_g.reshape(1, d), ln_b.reshape(1, d), tri3)
    return out.reshape(bsz, seq, d)


def _mlp_kernel(h_ref, wup_ref, wdn_ref, g_ref, b_ref, o_ref, *, alpha):
    h = h_ref[...]
    hb = h.astype(BF16)
    dff = wup_ref.shape[1]
    acc = None
    for f in range(0, dff, MLP_FF_CHUNK):
        u = jnp.dot(hb, wup_ref[:, f:f + MLP_FF_CHUNK], preferred_element_type=F32)
        u = jnp.maximum(u, 0.0)
        u = (u * u).astype(BF16)
        dn = jnp.dot(u, wdn_ref[f:f + MLP_FF_CHUNK, :], preferred_element_type=F32)
        acc = dn if acc is None else acc + dn
    o_ref[...] = _layer_norm(alpha * h + acc, g_ref[...], b_ref[...])


def _mlp_layer(h2, w_up, w_down, ln_g, ln_b, alpha):
    t, d = h2.shape
    dff = w_up.shape[1]
    tm = MLP_TOKENS
    return pl.pallas_call(
        functools.partial(_mlp_kernel, alpha=alpha),
        grid=(t // tm,),
        in_specs=[
            pl.BlockSpec((tm, d), lambda i: (i, 0)),
            _const_spec((d, dff)),
            _const_spec((dff, d)),
            _const_spec((1, d)),
            _const_spec((1, d)),
        ],
        out_specs=pl.BlockSpec((tm, d), lambda i: (i, 0)),
        out_shape=jax.ShapeDtypeStruct((t, d), F32),
        compiler_params=pltpu.CompilerParams(
            dimension_semantics=("arbitrary",), vmem_limit_bytes=VMEM_LIMIT),
        name="mlp_layer",
    )(h2, w_up.astype(BF16), w_down.astype(BF16), ln_g.reshape(1, d), ln_b.reshape(1, d))


def _sb_proj_kernel(h_ref, w_ref, *refs, with_vt):
    hb = h_ref[...].astype(BF16)
    d = h_ref.shape[1]
    res = jnp.dot(hb, w_ref[...], preferred_element_type=F32)
    out_refs = refs
    if with_vt:
        wvt_ref, out_refs, vt_ref = refs[0], refs[1:-1], refs[-1]
        vt = lax.dot_general(wvt_ref[...], hb, (((1,), (1,)), ((), ())), preferred_element_type=F32)
        vt_ref[...] = vt.astype(BF16)
    for n, o_ref in enumerate(out_refs):
        for hp in range(d // LANES):
            c0 = n * d + hp * LANES
            o_ref[hp] = res[:, c0:c0 + LANES].astype(BF16)


def _sb_proj(h, w, wvt=None):
    bsz, seq, d = h.shape
    n_out = w.shape[1] // d
    tm = PROJ_TOKENS
    hp = d // LANES
    out_sds = [jax.ShapeDtypeStruct((bsz, hp, seq, LANES), BF16)] * n_out
    out_specs = [pl.BlockSpec((None, hp, tm, LANES), lambda b, s: (b, 0, s, 0))] * n_out
    in_specs = [pl.BlockSpec((None, tm, d), lambda b, s: (b, s, 0)), _const_spec((d, n_out * d))]
    args = [h, w]
    if wvt is not None:
        in_specs.append(_const_spec((d, d)))
        args.append(wvt)
        out_sds.append(jax.ShapeDtypeStruct((bsz, d, seq), BF16))
        out_specs.append(pl.BlockSpec((None, d, tm), lambda b, s: (b, 0, s)))
    return pl.pallas_call(
        functools.partial(_sb_proj_kernel, with_vt=wvt is not None),
        grid=(bsz, seq // tm),
        in_specs=in_specs,
        out_specs=out_specs,
        out_shape=out_sds,
        compiler_params=pltpu.CompilerParams(
            dimension_semantics=("arbitrary", "arbitrary"), vmem_limit_bytes=VMEM_LIMIT),
        name="sb_proj",
    )(*args)


def _sb_logits(ks, q_blk, bias):
    z = lax.dot_general(ks, q_blk, (((1,), (1,)), ((), ())), preferred_element_type=F32)
    if bias is not None:
        z = z + bias
    t = jnp.maximum(z, jnp.log(1.0 + jnp.exp(jnp.minimum(z, SB_EXP_CLAMP))))
    return z, t.astype(BF16)


def _sb_weights(z, tb, aa_ref, r):
    ws, rs = [], []
    for hd in range(2):
        rows = slice(hd * SB_T, (hd + 1) * SB_T)
        cs = jnp.dot(aa_ref[...], tb[rows], preferred_element_type=F32)
        logw = z[rows] - cs[:SB_T]
        if r is not None:
            logw = logw - r[hd]
        ws.append(jnp.exp(logw).astype(BF16))
        rs.append(cs[SB_T:SB_T + 1])
    return jnp.concatenate(ws, axis=0), rs


def _sb_values(vt2, w):
    return jnp.dot(vt2, w, preferred_element_type=F32)


def _sb_tile(ks, q_blk, vt2, aa_ref, bias, r):
    z, tb = _sb_logits(ks, q_blk, bias)
    w, rs = _sb_weights(z, tb, aa_ref, r)
    return _sb_values(vt2, w), rs


def _sb_attn_kernel(q_ref, k_ref, vt_ref, aa_ref, bias_ref, o_ref, acc_ref, r_ref):
    n_blocks = q_ref.shape[0] // SB_T
    lane_k = lax.broadcasted_iota(jnp.int32, (SB_T, LANES), 1)
    row_v = lax.broadcasted_iota(jnp.int32, (LANES, SB_T), 0)

    def key_tile(k0):
        k = k_ref[pl.ds(k0, SB_T), :]
        vt = vt_ref[:, pl.ds(k0, SB_T)]
        zk, zv = jnp.zeros_like(k), jnp.zeros_like(vt)
        ks = jnp.concatenate([jnp.where(lane_k < SB_HEAD_DIM, k, zk),
                              jnp.where(lane_k >= SB_HEAD_DIM, k, zk)], axis=0)
        vt2 = jnp.concatenate([jnp.where(row_v < SB_HEAD_DIM, vt, zv),
                               jnp.where(row_v >= SB_HEAD_DIM, vt, zv)], axis=1)
        return ks, vt2

    def tile(q_blk, kv, bias, r):
        return _sb_tile(kv[0], q_blk, kv[1], aa_ref, bias, r)

    kvs = [key_tile(m * SB_T) for m in range(n_blocks)]
    tiles = [(m, j) for m in range(n_blocks) for j in ((m, m - 1) if m else (m,))]
    logits, weights, rsums, accs = {}, {}, {}, {}
    lowest_sums = None
    for i in range(len(tiles) + 2):
        if i < len(tiles):
            m, j = tiles[i]
            logits[i] = _sb_logits(kvs[j][0], q_ref[m * SB_T:(m + 1) * SB_T, :],
                                   bias_ref[...] if j == m else None)
        if 0 <= i - 1 < len(tiles):
            m, j = tiles[i - 1]
            weights[i - 1], rs = _sb_weights(*logits.pop(i - 1), aa_ref, rsums.get(m))
            rsums[m] = [a + b for a, b in zip(rsums[m], rs)] if m in rsums else rs
        if 0 <= i - 2 < len(tiles):
            m, j = tiles[i - 2]
            pv = _sb_values(kvs[j][1], weights.pop(i - 2))
            accs[m] = accs[m] + pv if m in accs else pv
            if j == max(m - 1, 0):
                o_ref[:, m * SB_T:(m + 1) * SB_T] = accs[m].astype(o_ref.dtype)
                if m > 1:
                    acc_ref[m] = accs[m]
                    r_ref[m] = jnp.concatenate(rsums[m], axis=0)
                    lowest_sums = rsums[m] if lowest_sums is None else [
                        jnp.minimum(a, b) for a, b in zip(lowest_sums, rsums[m])]

    def continue_block(m, carry):
        q0 = pl.multiple_of(m * SB_T, SB_T)

        def more(c):
            j, _, _, lowest = c
            return jnp.logical_and(j >= 0, lowest < SB_LOG_WEIGHT_FLOOR)

        def step(c):
            j, acc, rsum, _ = c
            pv, rs = tile(q_ref[pl.ds(q0, SB_T), :], key_tile(pl.multiple_of(j * SB_T, SB_T)),
                          None, [rsum[0:1], rsum[1:2]])
            rsum = rsum + jnp.concatenate(rs, axis=0)
            return j - 1, acc + pv, rsum, jnp.min(rsum)

        rsum = r_ref[m]
        lowest = jnp.min(rsum)

        @pl.when(lowest < SB_LOG_WEIGHT_FLOOR)
        def _():
            _, acc, _, _ = lax.while_loop(more, step, (m - 2, acc_ref[m], rsum, lowest))
            o_ref[:, pl.ds(q0, SB_T)] = acc.astype(o_ref.dtype)

        return carry

    if lowest_sums is not None:
        @pl.when(jnp.min(jnp.minimum(*lowest_sums)) < SB_LOG_WEIGHT_FLOOR)
        def _():
            lax.fori_loop(2, n_blocks, continue_block, 0)


def _sb_attention(q, k, vt):
    bsz, hp, seq, _ = q.shape
    ar = jnp.arange(SB_T)
    later = (ar[None, :] >= ar[:, None]).astype(BF16)
    aa = jnp.concatenate([later, jnp.ones((SB_SUM_ROWS, SB_T), BF16)], axis=0)
    causal = ar[:, None] < ar[None, :]
    bias = jnp.where(jnp.concatenate([causal, causal], axis=0), 0.0, SB_MASK_VALUE).astype(F32)
    slab = pl.BlockSpec((None, None, seq, LANES), lambda b, p: (b, p, 0, 0))
    slab_t = pl.BlockSpec((None, LANES, seq), lambda b, p: (b, p, 0))
    return pl.pallas_call(
        _sb_attn_kernel,
        grid=(bsz, hp),
        in_specs=[slab, slab, slab_t, _const_spec((SB_T + SB_SUM_ROWS, SB_T)),
                  _const_spec((2 * SB_T, SB_T))],
        out_specs=slab_t,
        out_shape=jax.ShapeDtypeStruct((bsz, hp * LANES, seq), BF16),
        scratch_shapes=[
            pltpu.VMEM((seq // SB_T, LANES, SB_T), F32),
            pltpu.VMEM((seq // SB_T, 2, SB_T), F32),
        ],
        compiler_params=pltpu.CompilerParams(
            dimension_semantics=("arbitrary", "arbitrary"), vmem_limit_bytes=VMEM_LIMIT),
        name="sb_attention",
    )(q, k, vt, aa, bias)


def _sb_out_kernel(h_ref, at_ref, wo_ref, g_ref, b_ref, o_ref, *, alpha):
    mix = lax.dot_general(at_ref[...], wo_ref[...], (((0,), (0,)), ((), ())),
                          preferred_element_type=F32)
    o_ref[...] = _layer_norm(alpha * h_ref[...] + mix, g_ref[...], b_ref[...])


def _sb_out(h, at, w_o, ln_g, ln_b, alpha):
    bsz, seq, d = h.shape
    tm = PROJ_TOKENS
    return pl.pallas_call(
        functools.partial(_sb_out_kernel, alpha=alpha),
        grid=(bsz, seq // tm),
        in_specs=[
            pl.BlockSpec((None, tm, d), lambda b, s: (b, s, 0)),
            pl.BlockSpec((None, d, tm), lambda b, s: (b, 0, s)),
            _const_spec((d, d)),
            _const_spec((1, d)),
            _const_spec((1, d)),
        ],
        out_specs=pl.BlockSpec((None, tm, d), lambda b, s: (b, s, 0)),
        out_shape=jax.ShapeDtypeStruct((bsz, seq, d), F32),
        compiler_params=pltpu.CompilerParams(
            dimension_semantics=("arbitrary", "arbitrary"), vmem_limit_bytes=VMEM_LIMIT),
        name="sb_out",
    )(h, at, w_o.astype(BF16), ln_g.reshape(1, d), ln_b.reshape(1, d))


def kernel(x, gla_w_in, gla_w_a2, gla_b_a, gla_norm_g, gla_w_o, kv_w, sb_w_q, sb_w_o,
           mlp_w_up, mlp_w_down, ln_g, ln_b):
    bsz, seq, d = x.shape
    depth = mlp_w_up.shape[0]
    n_a = gla_w_in.shape[0]
    alpha = (2.0 * depth) ** 0.25
    q_scale = SB_HEAD_DIM ** -0.5

    h = x
    k_sh = vt_sh = None
    for layer in range(depth):
        if layer < n_a:
            h = _gla_layer(h, gla_w_in[layer], gla_w_a2[layer], gla_b_a[layer], gla_norm_g[layer],
                           gla_w_o[layer], ln_g[layer, 0], ln_b[layer, 0], alpha)
        else:
            j = layer - n_a
            wq = (sb_w_q[j] * q_scale).astype(BF16)
            if j == 0:
                wqk = jnp.concatenate([wq, kv_w[:, :d].astype(BF16)], axis=1)
                q, k_sh, vt_sh = _sb_proj(h, wqk, kv_w[:, d:].T.astype(BF16))
            else:
                (q,) = _sb_proj(h, wq)
            att_t = _sb_attention(q, k_sh, vt_sh)
            h = _sb_out(h, att_t, sb_w_o[j], ln_g[layer, 0], ln_b[layer, 0], alpha)
        h = _mlp_layer(h.reshape(bsz * seq, d), mlp_w_up[layer], mlp_w_down[layer],
                       ln_g[layer, 1], ln_b[layer, 1], alpha).reshape(bsz, seq, d)
    return h
```

```python
import functools

import jax
import jax.numpy as jnp
from jax import lax
from jax.experimental import pallas as pl
from jax.experimental.pallas import tpu as pltpu

F32 = jnp.float32
BF16 = jnp.bfloat16

GLA_HEADS = 4
GATE_RANK = 16
GATE_TAU = 16.0
GLA_CHUNK = 64
SB_HEAD_DIM = 64
LN_EPS = 1e-5
RMS_EPS = 1e-6

LANES = 128
VMEM_LIMIT = 56 * 1024 * 1024

GLA_TOKENS = 256
GLA_PROJ_PIECE = 256
MLP_TOKENS = 512
MLP_ROW_GROUP = 512
MLP_FF_CHUNK = 1024
PROJ_TOKENS = 512
SB_T = 256
SB_SUM_ROWS = 16
SB_EXP_CLAMP = 88.0
SB_MASK_VALUE = -1e30
SB_LOG_WEIGHT_FLOOR = 88.0


def _const_spec(shape):
    nd = len(shape)
    return pl.BlockSpec(shape, lambda *_: (0,) * nd, pipeline_mode=pl.Buffered(1))


def _layer_norm(y, g, b):
    mu = jnp.mean(y, axis=-1, keepdims=True)
    yc = y - mu
    var = jnp.mean(yc * yc, axis=-1, keepdims=True)
    return yc * lax.rsqrt(var + LN_EPS) * g + b


def _log_sigmoid(x):
    return jnp.minimum(x, 0.0) - jnp.log(1.0 + jnp.exp(-jnp.abs(x)))


def _split3_bf16(x):
    hi = x.astype(BF16)
    r1 = x - hi.astype(F32)
    mid = r1.astype(BF16)
    lo = (r1 - mid.astype(F32)).astype(BF16)
    return hi, mid, lo


def _gla_kernel(h_ref, wqkvg_ref, wa1_ref, wa2_ref, ba_ref, ng_ref, wo_ref, lng_ref, lnb_ref,
                tri_ref, o_ref, state_ref, og_ref, proj0_ref, proj1_ref, a10_ref, a11_ref,
                hres0_ref, hres1_ref, hb_ref, *, alpha, dqk, dv, tiles_per_seq):
    g = pl.program_id(0)
    stash = ((proj0_ref, a10_ref, hres0_ref), (proj1_ref, a11_ref, hres1_ref))

    @pl.when(g == 0)
    def _():
        for ref in stash[1]:
            ref[...] = jnp.zeros_like(ref)

    @pl.when(lax.rem(g + tiles_per_seq - 1, tiles_per_seq) == 0)
    def _():
        state_ref[...] = jnp.zeros_like(state_ref)

    def step(cur, prev):
        pieces = _gla_projection_pieces(h_ref, wqkvg_ref, wa1_ref, hb_ref, *stash[cur])
        _gla_mix(*stash[prev], wa2_ref, ba_ref, ng_ref, wo_ref, lng_ref, lnb_ref, tri_ref,
                 o_ref, state_ref, og_ref, alpha=alpha, dqk=dqk, dv=dv,
                 between_units=lambda: next(pieces, None))
        for _ in pieces:
            pass

    @pl.when(lax.rem(g, 2) == 0)
    def _():
        step(0, 1)

    @pl.when(lax.rem(g, 2) == 1)
    def _():
        step(1, 0)


def _gla_projection_pieces(h_ref, wqkvg_ref, wa1_ref, hb_ref, proj_ref, a1_ref, hres_ref):
    h = h_ref[...]
    hres_ref[...] = h
    hb_ref[...] = h.astype(BF16)
    yield
    a1_ref[...] = jnp.dot(hb_ref[...], wa1_ref[...], preferred_element_type=F32)
    yield
    for c0 in range(0, wqkvg_ref.shape[1], GLA_PROJ_PIECE):
        cols = slice(c0, c0 + GLA_PROJ_PIECE)
        proj_ref[:, cols] = jnp.dot(hb_ref[...], wqkvg_ref[:, cols], preferred_element_type=F32)
        yield


def _gla_mix(proj_ref, a1_ref, hres_ref, wa2_ref, ba_ref, ng_ref, wo_ref, lng_ref, lnb_ref,
             tri_ref, o_ref, state_ref, og_ref, *, alpha, dqk, dv, between_units):
    hk = dqk // GLA_HEADS
    hv = dv // GLA_HEADS
    n_chunks = proj_ref.shape[0] // GLA_CHUNK

    for _ in range(4):
        between_units()
    gate = jnp.dot(a1_ref[...].astype(BF16), wa2_ref[...], preferred_element_type=F32) + ba_ref[...]
    log_a = _log_sigmoid(gate) * (1.0 / GATE_TAU)

    tri3 = tri_ref[...]
    row = lax.broadcasted_iota(jnp.int32, (GLA_CHUNK, GLA_CHUNK), 0)
    col = lax.broadcasted_iota(jnp.int32, (GLA_CHUNK, GLA_CHUNK), 1)
    causal = row >= col
    q_scale = hk ** -0.5

    b_chunks = []
    for c in range(n_chunks):
        hi, mid, lo = _split3_bf16(log_a[c * GLA_CHUNK:(c + 1) * GLA_CHUNK, :])
        b_chunks.append(jnp.dot(tri3, jnp.concatenate([hi, mid, lo], axis=0),
                                preferred_element_type=F32))

    def scores(c, hd):
        rows = slice(c * GLA_CHUNK, (c + 1) * GLA_CHUNK)
        q = proj_ref[rows, hd * hk:(hd + 1) * hk]
        k = proj_ref[rows, dqk + hd * hk:dqk + (hd + 1) * hk]
        b = b_chunks[c][:, hd * hk:(hd + 1) * hk]
        b_last = b[GLA_CHUNK - 1:GLA_CHUNK, :]
        q_dec = ((q * q_scale) * jnp.exp(b)).astype(BF16)
        k_dec = (k * jnp.exp(-b)).astype(BF16)
        k_end = (k * jnp.exp(b_last - b)).astype(BF16)
        att = lax.dot_general(q_dec, k_dec, (((1,), (1,)), ((), ())),
                              preferred_element_type=F32)
        return q_dec, k_end, b_last, att

    def finish(c, hd, q_dec, k_end, b_last, att):
        rows = slice(c * GLA_CHUNK, (c + 1) * GLA_CHUNK)
        v = proj_ref[rows, 2 * dqk + hd * hv:2 * dqk + (hd + 1) * hv]
        g = proj_ref[rows, 2 * dqk + dv + hd * hv:2 * dqk + dv + (hd + 1) * hv]
        vb = v.astype(BF16)
        att = jnp.where(causal, att, 0.0).astype(BF16)
        st = state_ref[hd]
        o = jnp.dot(jnp.concatenate([q_dec, att], axis=1),
                    jnp.concatenate([st.astype(BF16), vb], axis=0),
                    preferred_element_type=F32)
        u = lax.dot_general(k_end, vb, (((0,), (0,)), ((), ())),
                            preferred_element_type=F32)
        decay = jnp.exp(jnp.broadcast_to(b_last, (hk, hk)).T)
        state_ref[hd] = st * jnp.concatenate([decay] * (hv // hk), axis=1) + u
        o = o * lax.rsqrt(jnp.mean(o * o, axis=-1, keepdims=True) + RMS_EPS)
        o = o * ng_ref[...]
        o = o * (g / (1.0 + jnp.exp(-g)))
        og_ref[rows, hd * hv:(hd + 1) * hv] = o.astype(BF16)

    units = [(c, hd) for c in range(n_chunks) for hd in range(GLA_HEADS)]
    pending = scores(*units[0])
    for i, unit in enumerate(units):
        ahead = scores(*units[i + 1]) if i + 1 < len(units) else None
        if i % (len(units) // 8) == 0:
            between_units()
        finish(*unit, *pending)
        pending = ahead

    mix = jnp.dot(og_ref[...], wo_ref[...], preferred_element_type=F32)
    for _ in range(3):
        between_units()
    o_ref[...] = _layer_norm(alpha * hres_ref[...] + mix, lng_ref[...], lnb_ref[...])


def _gla_layer(h, w_in, w_a2, b_a, norm_g, w_o, ln_g, ln_b, alpha):
    bsz, seq, d = h.shape
    dqk = w_a2.shape[1]
    dv = w_o.shape[0]
    n_main = 2 * dqk + 2 * dv
    hv = dv // GLA_HEADS
    hk = dqk // GLA_HEADS
    wqkvg = w_in[:, :n_main].astype(BF16)
    wa1 = jnp.pad(w_in[:, n_main:], ((0, 0), (0, LANES - GATE_RANK))).astype(BF16)
    wa2 = jnp.pad(w_a2, ((0, LANES - GATE_RANK), (0, 0))).astype(BF16)
    tri = (jnp.arange(GLA_CHUNK)[:, None] >= jnp.arange(GLA_CHUNK)[None, :]).astype(BF16)
    tri3 = jnp.concatenate([tri, tri, tri], axis=1)
    ts = GLA_TOKENS
    n_tiles = bsz * seq // ts
    kern = functools.partial(_gla_kernel, alpha=alpha, dqk=dqk, dv=dv, tiles_per_seq=seq // ts)
    out = pl.pallas_call(
        kern,
        grid=(n_tiles + 1,),
        in_specs=[
            pl.BlockSpec((ts, d), lambda g: (jnp.minimum(g, n_tiles - 1), 0)),
            _const_spec((d, n_main)),
            _const_spec((d, LANES)),
            _const_spec((LANES, dqk)),
            _const_spec((1, dqk)),
            _const_spec((1, hv)),
            _const_spec((dv, d)),
            _const_spec((1, d)),
            _const_spec((1, d)),
            _const_spec((GLA_CHUNK, 3 * GLA_CHUNK)),
        ],
        out_specs=pl.BlockSpec((ts, d), lambda g: (jnp.maximum(g - 1, 0), 0)),
        out_shape=jax.ShapeDtypeStruct((bsz * seq, d), F32),
        scratch_shapes=[
            pltpu.VMEM((GLA_HEADS, hk, hv), F32),
            pltpu.VMEM((ts, dv), BF16),
            pltpu.VMEM((ts, n_main), F32), pltpu.VMEM((ts, n_main), F32),
            pltpu.VMEM((ts, LANES), F32), pltpu.VMEM((ts, LANES), F32),
            pltpu.VMEM((ts, d), F32), pltpu.VMEM((ts, d), F32),
            pltpu.VMEM((ts, d), BF16),
        ],
        compiler_params=pltpu.CompilerParams(
            dimension_semantics=("arbitrary",), vmem_limit_bytes=VMEM_LIMIT),
        name="gla_layer",
    )(h.reshape(bsz * seq, d), wqkvg, wa1, wa2, b_a.reshape(1, dqk), norm_g.reshape(1, hv),
      w_o.astype(BF16), ln_g.reshape(1, d), ln_b.reshape(1, d), tri3)
    return out.reshape(bsz, seq, d)


def _mlp_kernel(*refs, alpha, with_mixer, with_q):
    refs = list(refs)
    h_ref = refs.pop(0)
    if with_mixer:
        at_ref, wo_ref, g1_ref, b1_ref = (refs.pop(0) for _ in range(4))
    wup_ref, wdn_ref, g_ref, b_ref = (refs.pop(0) for _ in range(4))
    if with_q:
        wq_ref = refs.pop(0)
    o_ref = refs.pop(0)
    q_ref = refs.pop(0) if with_q else None

    for r0 in range(0, h_ref.shape[0], MLP_ROW_GROUP):
        rows = slice(r0, r0 + MLP_ROW_GROUP)
        h = h_ref[rows, :]
        if with_mixer:
            mix = lax.dot_general(at_ref[:, rows], wo_ref[...], (((0,), (0,)), ((), ())),
                                  preferred_element_type=F32)
            h = _layer_norm(alpha * h + mix, g1_ref[...], b1_ref[...])
        hb = h.astype(BF16)
        dff = wup_ref.shape[1]
        acc = None
        for f in range(0, dff, MLP_FF_CHUNK):
            u = jnp.dot(hb, wup_ref[:, f:f + MLP_FF_CHUNK], preferred_element_type=F32)
            u = jnp.maximum(u, 0.0)
            u = (u * u).astype(BF16)
            dn = jnp.dot(u, wdn_ref[f:f + MLP_FF_CHUNK, :], preferred_element_type=F32)
            acc = dn if acc is None else acc + dn
        out = _layer_norm(alpha * h + acc, g_ref[...], b_ref[...])
        o_ref[rows, :] = out
        if with_q:
            q = jnp.dot(out.astype(BF16), wq_ref[...], preferred_element_type=F32)
            for hp in range(q_ref.shape[0]):
                q_ref[hp, rows, :] = q[:, hp * LANES:(hp + 1) * LANES].astype(BF16)


def _mlp_layer(h, w_up, w_down, ln_g, ln_b, alpha, mixer=None, wq=None):
    bsz, seq, d = h.shape
    dff = w_up.shape[1]
    tm = MLP_TOKENS
    row = lambda v: v.reshape(1, d)
    tile = pl.BlockSpec((None, tm, d), lambda b, s: (b, s, 0))
    in_specs, args = [tile], [h]
    if mixer is not None:
        at, w_o, g1, b1 = mixer
        in_specs += [pl.BlockSpec((None, d, tm), lambda b, s: (b, 0, s)), _const_spec((d, d)),
                     _const_spec((1, d)), _const_spec((1, d))]
        args += [at, w_o.astype(BF16), row(g1), row(b1)]
    in_specs += [_const_spec((d, dff)), _const_spec((dff, d)), _const_spec((1, d)), _const_spec((1, d))]
    args += [w_up.astype(BF16), w_down.astype(BF16), row(ln_g), row(ln_b)]
    out_specs, out_shape = [tile], [jax.ShapeDtypeStruct((bsz, seq, d), F32)]
    if wq is not None:
        in_specs.append(_const_spec((d, d)))
        args.append(wq)
        out_specs.append(pl.BlockSpec((None, d // LANES, tm, LANES), lambda b, s: (b, 0, s, 0)))
        out_shape.append(jax.ShapeDtypeStruct((bsz, d // LANES, seq, LANES), BF16))
    return pl.pallas_call(
        functools.partial(_mlp_kernel, alpha=alpha, with_mixer=mixer is not None,
                          with_q=wq is not None),
        grid=(bsz, seq // tm),
        in_specs=in_specs,
        out_specs=out_specs,
        out_shape=out_shape,
        compiler_params=pltpu.CompilerParams(
            dimension_semantics=("arbitrary", "arbitrary"), vmem_limit_bytes=VMEM_LIMIT),
        name="mlp_layer",
    )(*args)


def _sb_proj_kernel(h_ref, w_ref, *refs, with_vt):
    hb = h_ref[...].astype(BF16)
    d = h_ref.shape[1]
    res = jnp.dot(hb, w_ref[...], preferred_element_type=F32)
    out_refs = refs
    if with_vt:
        wvt_ref, out_refs, vt_ref = refs[0], refs[1:-1], refs[-1]
        vt = lax.dot_general(wvt_ref[...], hb, (((1,), (1,)), ((), ())), preferred_element_type=F32)
        vt_ref[...] = vt.astype(BF16)
    for n, o_ref in enumerate(out_refs):
        for hp in range(d // LANES):
            c0 = n * d + hp * LANES
            o_ref[hp] = res[:, c0:c0 + LANES].astype(BF16)


def _sb_proj(h, w, wvt=None):
    bsz, seq, d = h.shape
    n_out = w.shape[1] // d
    tm = PROJ_TOKENS
    hp = d // LANES
    out_sds = [jax.ShapeDtypeStruct((bsz, hp, seq, LANES), BF16)] * n_out
    out_specs = [pl.BlockSpec((None, hp, tm, LANES), lambda b, s: (b, 0, s, 0))] * n_out
    in_specs = [pl.BlockSpec((None, tm, d), lambda b, s: (b, s, 0)), _const_spec((d, n_out * d))]
    args = [h, w]
    if wvt is not None:
        in_specs.append(_const_spec((d, d)))
        args.append(wvt)
        out_sds.append(jax.ShapeDtypeStruct((bsz, d, seq), BF16))
        out_specs.append(pl.BlockSpec((None, d, tm), lambda b, s: (b, 0, s)))
    return pl.pallas_call(
        functools.partial(_sb_proj_kernel, with_vt=wvt is not None),
        grid=(bsz, seq // tm),
        in_specs=in_specs,
        out_specs=out_specs,
        out_shape=out_sds,
        compiler_params=pltpu.CompilerParams(
            dimension_semantics=("arbitrary", "arbitrary"), vmem_limit_bytes=VMEM_LIMIT),
        name="sb_proj",
    )(*args)


def _sb_logits(ks, q_blk, bias):
    z = lax.dot_general(ks, q_blk, (((1,), (1,)), ((), ())), preferred_element_type=F32)

    def softplus(x):
        return jnp.maximum(x, jnp.log(1.0 + jnp.exp(jnp.minimum(x, SB_EXP_CLAMP)))).astype(BF16)

    if bias is None:
        return z, softplus(z)
    half = SB_T // 2
    zs, ts = [], []
    for hd in range(2):
        r0 = hd * SB_T
        za = z[r0:r0 + half, :half] + bias[r0:r0 + half, :half]
        zb = z[r0:r0 + SB_T, half:] + bias[r0:r0 + SB_T, half:]
        ta = jnp.concatenate([softplus(za), jnp.zeros((half, half), BF16)], axis=0)
        zs.append((za, zb))
        ts.append(jnp.concatenate([ta, softplus(zb)], axis=1))
    return zs, jnp.concatenate(ts, axis=0)


def _sb_weights(z, tb, aa_ref, r):
    half = SB_T // 2
    ws, rs = [], []
    for hd in range(2):
        rows = slice(hd * SB_T, (hd + 1) * SB_T)
        cs = jnp.dot(aa_ref[...], tb[rows], preferred_element_type=F32)
        if r is None:
            za, zb = z[hd]
            wa = jnp.exp(za - cs[:half, :half]).astype(BF16)
            wb = jnp.exp(zb - cs[:SB_T, half:]).astype(BF16)
            wa = jnp.concatenate([wa, jnp.zeros((half, half), BF16)], axis=0)
            ws.append(jnp.concatenate([wa, wb], axis=1))
        else:
            ws.append(jnp.exp(z[rows] - cs[:SB_T] - r[hd]).astype(BF16))
        rs.append(cs[SB_T:SB_T + 1])
    return jnp.concatenate(ws, axis=0), rs


def _sb_values(vt2, w):
    return jnp.dot(vt2, w, preferred_element_type=F32)


def _sb_tile(ks, q_blk, vt2, aa_ref, bias, r):
    z, tb = _sb_logits(ks, q_blk, bias)
    w, rs = _sb_weights(z, tb, aa_ref, r)
    return _sb_values(vt2, w), rs


def _sb_attn_kernel(q_ref, k_ref, vt_ref, aa_ref, bias_ref, o_ref, acc_ref, r_ref):
    n_blocks = q_ref.shape[0] // SB_T
    lane_k = lax.broadcasted_iota(jnp.int32, (SB_T, LANES), 1)
    row_v = lax.broadcasted_iota(jnp.int32, (LANES, SB_T), 0)

    def key_tile(k0):
        k = k_ref[pl.ds(k0, SB_T), :]
        vt = vt_ref[:, pl.ds(k0, SB_T)]
        zk, zv = jnp.zeros_like(k), jnp.zeros_like(vt)
        ks = jnp.concatenate([jnp.where(lane_k < SB_HEAD_DIM, k, zk),
                              jnp.where(lane_k >= SB_HEAD_DIM, k, zk)], axis=0)
        vt2 = jnp.concatenate([jnp.where(row_v < SB_HEAD_DIM, vt, zv),
                               jnp.where(row_v >= SB_HEAD_DIM, vt, zv)], axis=1)
        return ks, vt2

    def tile(q_blk, kv, bias, r):
        return _sb_tile(kv[0], q_blk, kv[1], aa_ref, bias, r)

    kvs = [key_tile(m * SB_T) for m in range(n_blocks)]
    tiles = [(m, j) for m in range(n_blocks) for j in ((m, m - 1) if m else (m,))]
    logits, weights, rsums, accs = {}, {}, {}, {}
    lowest_sums = None
    for i in range(len(tiles) + 2):
        if i < len(tiles):
            m, j = tiles[i]
            logits[i] = _sb_logits(kvs[j][0], q_ref[m * SB_T:(m + 1) * SB_T, :],
                                   bias_ref[...] if j == m else None)
        if 0 <= i - 1 < len(tiles):
            m, j = tiles[i - 1]
            weights[i - 1], rs = _sb_weights(*logits.pop(i - 1), aa_ref, rsums.get(m))
            rsums[m] = [a + b for a, b in zip(rsums[m], rs)] if m in rsums else rs
        if 0 <= i - 2 < len(tiles):
            m, j = tiles[i - 2]
            pv = _sb_values(kvs[j][1], weights.pop(i - 2))
            accs[m] = accs[m] + pv if m in accs else pv
            if j == max(m - 1, 0):
                o_ref[:, m * SB_T:(m + 1) * SB_T] = accs[m].astype(o_ref.dtype)
                if m > 1:
                    acc_ref[m] = accs[m]
                    r_ref[m] = jnp.concatenate(rsums[m], axis=0)
                    lowest_sums = rsums[m] if lowest_sums is None else [
                        jnp.minimum(a, b) for a, b in zip(lowest_sums, rsums[m])]

    def continue_block(m, carry):
        q0 = pl.multiple_of(m * SB_T, SB_T)

        def more(c):
            j, _, _, lowest = c
            return jnp.logical_and(j >= 0, lowest < SB_LOG_WEIGHT_FLOOR)

        def step(c):
            j, acc, rsum, _ = c
            pv, rs = tile(q_ref[pl.ds(q0, SB_T), :], key_tile(pl.multiple_of(j * SB_T, SB_T)),
                          None, [rsum[0:1], rsum[1:2]])
            rsum = rsum + jnp.concatenate(rs, axis=0)
            return j - 1, acc + pv, rsum, jnp.min(rsum)

        rsum = r_ref[m]
        lowest = jnp.min(rsum)

        @pl.when(lowest < SB_LOG_WEIGHT_FLOOR)
        def _():
            _, acc, _, _ = lax.while_loop(more, step, (m - 2, acc_ref[m], rsum, lowest))
            o_ref[:, pl.ds(q0, SB_T)] = acc.astype(o_ref.dtype)

        return carry

    if lowest_sums is not None:
        @pl.when(jnp.min(jnp.minimum(*lowest_sums)) < SB_LOG_WEIGHT_FLOOR)
        def _():
            lax.fori_loop(2, n_blocks, continue_block, 0)


def _sb_attention(q, k, vt):
    bsz, hp, seq, _ = q.shape
    ar = jnp.arange(SB_T)
    later = (ar[None, :] >= ar[:, None]).astype(BF16)
    aa = jnp.concatenate([later, jnp.ones((SB_SUM_ROWS, SB_T), BF16)], axis=0)
    causal = ar[:, None] < ar[None, :]
    bias = jnp.where(jnp.concatenate([causal, causal], axis=0), 0.0, SB_MASK_VALUE).astype(F32)
    slab = pl.BlockSpec((None, None, seq, LANES), lambda b, p: (b, p, 0, 0))
    slab_t = pl.BlockSpec((None, LANES, seq), lambda b, p: (b, p, 0))
    return pl.pallas_call(
        _sb_attn_kernel,
        grid=(bsz, hp),
        in_specs=[slab, slab, slab_t, _const_spec((SB_T + SB_SUM_ROWS, SB_T)),
                  _const_spec((2 * SB_T, SB_T))],
        out_specs=slab_t,
        out_shape=jax.ShapeDtypeStruct((bsz, hp * LANES, seq), BF16),
        scratch_shapes=[
            pltpu.VMEM((seq // SB_T, LANES, SB_T), F32),
            pltpu.VMEM((seq // SB_T, 2, SB_T), F32),
        ],
        compiler_params=pltpu.CompilerParams(
            dimension_semantics=("arbitrary", "arbitrary"), vmem_limit_bytes=VMEM_LIMIT),
        name="sb_attention",
    )(q, k, vt, aa, bias)


def kernel(x, gla_w_in, gla_w_a2, gla_b_a, gla_norm_g, gla_w_o, kv_w, sb_w_q, sb_w_o,
           mlp_w_up, mlp_w_down, ln_g, ln_b):
    bsz, seq, d = x.shape
    depth = mlp_w_up.shape[0]
    n_a = gla_w_in.shape[0]
    alpha = (2.0 * depth) ** 0.25
    q_scale = SB_HEAD_DIM ** -0.5

    h = x
    q = k_sh = vt_sh = None
    for layer in range(depth):
        mixer = None
        if layer < n_a:
            h = _gla_layer(h, gla_w_in[layer], gla_w_a2[layer], gla_b_a[layer], gla_norm_g[layer],
                           gla_w_o[layer], ln_g[layer, 0], ln_b[layer, 0], alpha)
        else:
            j = layer - n_a
            if j == 0:
                wq = (sb_w_q[j] * q_scale).astype(BF16)
                wqk = jnp.concatenate([wq, kv_w[:, :d].astype(BF16)], axis=1)
                q, k_sh, vt_sh = _sb_proj(h, wqk, kv_w[:, d:].T.astype(BF16))
            mixer = (_sb_attention(q, k_sh, vt_sh), sb_w_o[j], ln_g[layer, 0], ln_b[layer, 0])
        next_wq = None
        if n_a <= layer < depth - 1:
            next_wq = (sb_w_q[layer + 1 - n_a] * q_scale).astype(BF16)
        outs = _mlp_layer(h, mlp_w_up[layer], mlp_w_down[layer], ln_g[layer, 1], ln_b[layer, 1],
                          alpha, mixer=mixer, wq=next_wq)
        h = outs[0]
        if next_wq is not None:
            q = outs[1]
    return h
```

```python
import functools

import jax
import jax.numpy as jnp
from jax import lax
from jax.experimental import pallas as pl
from jax.experimental.pallas import tpu as pltpu

F32 = jnp.float32
BF16 = jnp.bfloat16

GLA_HEADS = 4
GATE_RANK = 16
GATE_TAU = 16.0
GLA_CHUNK = 64
SB_HEAD_DIM = 64
LN_EPS = 1e-5
RMS_EPS = 1e-6

LANES = 128
VMEM_LIMIT = 56 * 1024 * 1024

GLA_TOKENS = 512
GLA_PROJ_PIECE = 256
MLP_TOKENS = 512
MLP_ANCHOR_CHUNKS = (1, 2)
MLP_FF_CHUNK = 1024
PROJ_TOKENS = 1024
SB_T = 256
SB_SUM_ROWS = 16
SB_EXP_CLAMP = 88.0
SB_MASK_VALUE = -1e30
SB_LOG_WEIGHT_FLOOR = 88.0


def _const_spec(shape):
    nd = len(shape)
    return pl.BlockSpec(shape, lambda *_: (0,) * nd, pipeline_mode=pl.Buffered(1))


def _layer_norm(y, g, b):
    mu = jnp.mean(y, axis=-1, keepdims=True)
    yc = y - mu
    var = jnp.mean(yc * yc, axis=-1, keepdims=True)
    return yc * lax.rsqrt(var + LN_EPS) * g + b


def _log_sigmoid(x):
    return jnp.minimum(x, 0.0) - jnp.log(1.0 + jnp.exp(-jnp.abs(x)))


def _split3_bf16(x):
    hi = x.astype(BF16)
    r1 = x - hi.astype(F32)
    mid = r1.astype(BF16)
    lo = (r1 - mid.astype(F32)).astype(BF16)
    return hi, mid, lo


def _gla_kernel(h_ref, hprev_ref, wqkvg_ref, wa1_ref, wa2_ref, ba_ref, ng_ref, wo_ref, lng_ref,
                lnb_ref, tri_ref, o_ref, state_ref, og_ref, proj0_ref, proj1_ref, a10_ref, a11_ref,
                hb_ref, *, alpha, dqk, dv, tiles_per_seq):
    g = pl.program_id(0)
    stash = ((proj0_ref, a10_ref), (proj1_ref, a11_ref))

    @pl.when(g == 0)
    def _():
        for ref in stash[1]:
            ref[...] = jnp.zeros_like(ref)

    @pl.when(lax.rem(g + tiles_per_seq - 1, tiles_per_seq) == 0)
    def _():
        state_ref[...] = jnp.zeros_like(state_ref)

    def step(cur, prev):
        pieces = _gla_projection_pieces(h_ref, wqkvg_ref, wa1_ref, hb_ref, *stash[cur])
        _gla_mix(*stash[prev], hprev_ref, wa2_ref, ba_ref, ng_ref, wo_ref, lng_ref, lnb_ref, tri_ref,
                 o_ref, state_ref, og_ref, alpha=alpha, dqk=dqk, dv=dv,
                 between_units=lambda: next(pieces, None))
        for _ in pieces:
            pass

    @pl.when(lax.rem(g, 2) == 0)
    def _():
        step(0, 1)

    @pl.when(lax.rem(g, 2) == 1)
    def _():
        step(1, 0)


def _gla_projection_pieces(h_ref, wqkvg_ref, wa1_ref, hb_ref, proj_ref, a1_ref):
    hb_ref[...] = h_ref[...].astype(BF16)
    yield
    a1_ref[...] = jnp.dot(hb_ref[...], wa1_ref[...], preferred_element_type=F32)
    yield
    for c0 in range(0, proj_ref.shape[1], GLA_PROJ_PIECE):
        cols = slice(c0, c0 + GLA_PROJ_PIECE)
        proj_ref[:, cols] = jnp.dot(hb_ref[...], wqkvg_ref[:, cols], preferred_element_type=F32)
        yield


def _gla_mix(proj_ref, a1_ref, hres_ref, wa2_ref, ba_ref, ng_ref, wo_ref, lng_ref, lnb_ref,
             tri_ref, o_ref, state_ref, og_ref, *, alpha, dqk, dv, between_units):
    hk = dqk // GLA_HEADS
    hv = dv // GLA_HEADS
    n_chunks = proj_ref.shape[0] // GLA_CHUNK

    gate = jnp.dot(a1_ref[...].astype(BF16), wa2_ref[...], preferred_element_type=F32) + ba_ref[...]
    for _ in range(4):
        between_units()
    log_a = _log_sigmoid(gate) * (1.0 / GATE_TAU)

    tri3 = tri_ref[...]
    row = lax.broadcasted_iota(jnp.int32, (GLA_CHUNK, GLA_CHUNK), 0)
    col = lax.broadcasted_iota(jnp.int32, (GLA_CHUNK, GLA_CHUNK), 1)
    causal = row >= col
    q_scale = hk ** -0.5

    b_chunks = []
    for c in range(n_chunks):
        hi, mid, lo = _split3_bf16(log_a[c * GLA_CHUNK:(c + 1) * GLA_CHUNK, :])
        b_chunks.append(jnp.dot(tri3, jnp.concatenate([hi, mid, lo], axis=0),
                                preferred_element_type=F32))

    def scores(c, hd):
        rows = slice(c * GLA_CHUNK, (c + 1) * GLA_CHUNK)
        q = proj_ref[rows, hd * hk:(hd + 1) * hk]
        k = proj_ref[rows, dqk + hd * hk:dqk + (hd + 1) * hk]
        b = b_chunks[c][:, hd * hk:(hd + 1) * hk]
        b_last = b[GLA_CHUNK - 1:GLA_CHUNK, :]
        q_dec = ((q * q_scale) * jnp.exp(b)).astype(BF16)
        k_dec = (k * jnp.exp(-b)).astype(BF16)
        k_end = (k * jnp.exp(b_last - b)).astype(BF16)
        att = lax.dot_general(q_dec, k_dec, (((1,), (1,)), ((), ())),
                              preferred_element_type=F32)
        return q_dec, k_end, b_last, att

    def finish(c, hd, q_dec, k_end, b_last, att):
        rows = slice(c * GLA_CHUNK, (c + 1) * GLA_CHUNK)
        v = proj_ref[rows, 2 * dqk + hd * hv:2 * dqk + (hd + 1) * hv]
        g = proj_ref[rows, 2 * dqk + dv + hd * hv:2 * dqk + dv + (hd + 1) * hv]
        vb = v.astype(BF16)
        att = jnp.where(causal, att, 0.0).astype(BF16)
        st = state_ref[hd]
        o = jnp.dot(jnp.concatenate([q_dec, att], axis=1),
                    jnp.concatenate([st.astype(BF16), vb], axis=0),
                    preferred_element_type=F32)
        u = lax.dot_general(k_end, vb, (((0,), (0,)), ((), ())),
                            preferred_element_type=F32)
        decay = jnp.exp(jnp.broadcast_to(b_last, (hk, hk)).T)
        state_ref[hd] = st * jnp.concatenate([decay] * (hv // hk), axis=1) + u
        o = o * lax.rsqrt(jnp.mean(o * o, axis=-1, keepdims=True) + RMS_EPS)
        o = o * ng_ref[...]
        o = o * (g / (1.0 + jnp.exp(-g)))
        og_ref[rows, hd * hv:(hd + 1) * hv] = o.astype(BF16)

    units = [(c, hd) for c in range(n_chunks) for hd in range(GLA_HEADS)]
    pending = scores(*units[0])
    for i, unit in enumerate(units):
        ahead = scores(*units[i + 1]) if i + 1 < len(units) else None
        if i % (len(units) // 8) == 0:
            between_units()
        finish(*unit, *pending)
        pending = ahead

    mix = jnp.dot(og_ref[...], wo_ref[...], preferred_element_type=F32)
    for _ in range(3):
        between_units()
    o_ref[...] = _layer_norm(alpha * hres_ref[...] + mix, lng_ref[...], lnb_ref[...])


def _gla_layer(h, w_in, w_a2, b_a, norm_g, w_o, ln_g, ln_b, alpha):
    bsz, seq, d = h.shape
    dqk = w_a2.shape[1]
    dv = w_o.shape[0]
    n_main = 2 * dqk + 2 * dv
    hv = dv // GLA_HEADS
    hk = dqk // GLA_HEADS
    wqkvg = w_in.astype(BF16)
    wa1 = jnp.pad(w_in[:, n_main:], ((0, 0), (0, LANES - GATE_RANK))).astype(BF16)
    wa2 = jnp.pad(w_a2, ((0, LANES - GATE_RANK), (0, 0))).astype(BF16)
    tri = (jnp.arange(GLA_CHUNK)[:, None] >= jnp.arange(GLA_CHUNK)[None, :]).astype(BF16)
    tri3 = jnp.concatenate([tri, tri, tri], axis=1)
    ts = GLA_TOKENS
    n_tiles = bsz * seq // ts
    kern = functools.partial(_gla_kernel, alpha=alpha, dqk=dqk, dv=dv, tiles_per_seq=seq // ts)
    out = pl.pallas_call(
        kern,
        grid=(n_tiles + 1,),
        in_specs=[
            pl.BlockSpec((ts, d), lambda g: (jnp.minimum(g, n_tiles - 1), 0)),
            pl.BlockSpec((ts, d), lambda g: (jnp.maximum(g - 1, 0), 0)),
            _const_spec(w_in.shape),
            _const_spec((d, LANES)),
            _const_spec((LANES, dqk)),
            _const_spec((1, dqk)),
            _const_spec((1, hv)),
            _const_spec((dv, d)),
            _const_spec((1, d)),
            _const_spec((1, d)),
            _const_spec((GLA_CHUNK, 3 * GLA_CHUNK)),
        ],
        out_specs=pl.BlockSpec((ts, d), lambda g: (jnp.maximum(g - 1, 0), 0)),
        out_shape=jax.ShapeDtypeStruct((bsz * seq, d), F32),
        scratch_shapes=[
            pltpu.VMEM((GLA_HEADS, hk, hv), F32),
            pltpu.VMEM((ts, dv), BF16),
            pltpu.VMEM((ts, n_main), F32), pltpu.VMEM((ts, n_main), F32),
            pltpu.VMEM((ts, LANES), F32), pltpu.VMEM((ts, LANES), F32),
            pltpu.VMEM((ts, d), BF16),
        ],
        compiler_params=pltpu.CompilerParams(
            dimension_semantics=("arbitrary",), vmem_limit_bytes=VMEM_LIMIT),
        name="gla_layer",
    )(h.reshape(bsz * seq, d), h.reshape(bsz * seq, d), wqkvg, wa1, wa2, b_a.reshape(1, dqk), norm_g.reshape(1, hv),
      w_o.astype(BF16), ln_g.reshape(1, d), ln_b.reshape(1, d), tri3)
    return out.reshape(bsz, seq, d)


def _mlp_kernel(*refs, alpha, with_mixer, with_q, n_tiles):
    refs = list(refs)
    h_ref = refs.pop(0)
    if with_mixer:
        at_ref, wo_ref, g1_ref, b1_ref = (refs.pop(0) for _ in range(4))
    wup_ref, wdn_ref, g_ref, b_ref = (refs.pop(0) for _ in range(4))
    wq_ref = refs.pop(0) if with_q else None
    o_ref = refs.pop(0)
    q_ref = refs.pop(0) if with_q else None
    pre_refs = (refs.pop(0), refs.pop(0))
    if with_mixer:
        h1_refs = (refs.pop(0), refs.pop(0))
        hb_refs = (refs.pop(0), refs.pop(0))
    g = pl.program_id(0)

    def mlp(hb, anchors):
        acc = None
        for i, f in enumerate(range(0, wup_ref.shape[1], MLP_FF_CHUNK)):
            u = jnp.dot(hb, wup_ref[:, f:f + MLP_FF_CHUNK], preferred_element_type=F32)
            u = jnp.maximum(u, 0.0)
            u = (u * u).astype(BF16)
            if i in anchors:
                u = jnp.where(g >= 0, u, anchors[i])
            dn = jnp.dot(u, wdn_ref[f:f + MLP_FF_CHUNK, :], preferred_element_type=F32)
            acc = dn if acc is None else acc + dn
        return acc

    def step(p, run_a, run_b, run_c):
        anchors = {}
        if with_mixer:
            pre_in, pre_out = pre_refs[p], pre_refs[1 - p]
        else:
            pre_in, pre_out = pre_refs[1 - p], pre_refs[p]
        if run_a:
            mix = lax.dot_general(at_ref[...], wo_ref[...], (((0,), (0,)), ((), ())),
                                  preferred_element_type=F32)
        if run_c:
            out = _layer_norm(pre_in[...], g_ref[...], b_ref[...])
            o_ref[...] = out
            out_b = out.astype(BF16)
            anchors[MLP_ANCHOR_CHUNKS[0]] = out_b
        if run_a:
            h1 = _layer_norm(alpha * h_ref[...] + mix, g1_ref[...], b1_ref[...])
            h1_b = h1.astype(BF16)
            h1_refs[p][...] = h1
            hb_refs[p][...] = h1_b
            anchors[MLP_ANCHOR_CHUNKS[1]] = h1_b
        if run_b and with_mixer:
            pre_out[...] = alpha * h1_refs[1 - p][...] + mlp(hb_refs[1 - p][...], anchors)
        elif run_b:
            h = h_ref[...]
            pre_out[...] = alpha * h + mlp(h.astype(BF16), anchors)
        if run_c and with_q:
            q = jnp.dot(out_b, wq_ref[...], preferred_element_type=F32)
            for hp in range(q_ref.shape[0]):
                q_ref[hp] = q[:, hp * LANES:(hp + 1) * LANES].astype(BF16)

    first_full = 2 if with_mixer else 1
    for p in range(2):
        @pl.when(jnp.logical_and(jnp.logical_and(g >= first_full, g < n_tiles),
                                 lax.rem(g, 2) == p))
        def _(p=p):
            step(p, with_mixer, True, True)

    if with_mixer:
        edge_steps = {0: (True, False, False), 1: (True, True, False),
                      n_tiles: (False, True, True), n_tiles + 1: (False, False, True)}
    else:
        edge_steps = {0: (False, True, False), n_tiles: (False, False, True)}
    for g_edge, stages in edge_steps.items():
        @pl.when(g == g_edge)
        def _(g_edge=g_edge, stages=stages):
            step(g_edge % 2, *stages)


def _mlp_layer(h, w_up, w_down, ln_g, ln_b, alpha, mixer=None, wq=None):
    bsz, seq, d = h.shape
    dff = w_up.shape[1]
    tm = MLP_TOKENS
    tiles_per_seq = seq // tm
    n_tiles = bsz * tiles_per_seq
    lag = 2 if mixer is not None else 1
    assert n_tiles > lag, "the staged MLP kernel needs more token tiles than stages"
    row = lambda v: v.reshape(1, d)
    tile_in = lambda g: jnp.minimum(g, n_tiles - 1)
    tile_out = lambda g: jnp.maximum(g - lag, 0)
    in_specs, args = [pl.BlockSpec((tm, d), lambda g: (tile_in(g), 0))], [h.reshape(bsz * seq, d)]
    if mixer is not None:
        at, w_o, g1, b1 = mixer
        in_specs += [pl.BlockSpec((None, d, tm), lambda g: (tile_in(g) // tiles_per_seq, 0,
                                                            tile_in(g) % tiles_per_seq)),
                     _const_spec((d, d)), _const_spec((1, d)), _const_spec((1, d))]
        args += [at, w_o.astype(BF16), row(g1), row(b1)]
    in_specs += [_const_spec((d, dff)), _const_spec((dff, d)), _const_spec((1, d)), _const_spec((1, d))]
    args += [w_up.astype(BF16), w_down.astype(BF16), row(ln_g), row(ln_b)]
    out_specs = [pl.BlockSpec((tm, d), lambda g: (tile_out(g), 0))]
    out_shape = [jax.ShapeDtypeStruct((bsz * seq, d), F32)]
    if wq is not None:
        in_specs.append(_const_spec((d, d)))
        args.append(wq)
        out_specs.append(pl.BlockSpec((None, d // LANES, tm, LANES),
                                      lambda g: (tile_out(g) // tiles_per_seq, 0,
                                                 tile_out(g) % tiles_per_seq, 0)))
        out_shape.append(jax.ShapeDtypeStruct((bsz, d // LANES, seq, LANES), BF16))
    scratch = [pltpu.VMEM((tm, d), F32)] * 2
    if mixer is not None:
        scratch += [pltpu.VMEM((tm, d), F32)] * 2
        scratch += [pltpu.VMEM((tm, d), BF16)] * 2
    outs = pl.pallas_call(
        functools.partial(_mlp_kernel, alpha=alpha, with_mixer=mixer is not None,
                          with_q=wq is not None, n_tiles=n_tiles),
        grid=(n_tiles + lag,),
        in_specs=in_specs,
        out_specs=out_specs,
        out_shape=out_shape,
        scratch_shapes=scratch,
        compiler_params=pltpu.CompilerParams(
            dimension_semantics=("arbitrary",), vmem_limit_bytes=VMEM_LIMIT),
        name="mlp_layer",
    )(*args)
    return [outs[0].reshape(bsz, seq, d)] + list(outs[1:])


def _sb_proj_kernel(h_ref, w_ref, *refs, with_vt):
    hb = h_ref[...].astype(BF16)
    d = h_ref.shape[1]
    res = jnp.dot(hb, w_ref[...], preferred_element_type=F32)
    out_refs = refs
    if with_vt:
        wvt_ref, out_refs, vt_ref = refs[0], refs[1:-1], refs[-1]
        vt = lax.dot_general(wvt_ref[...], hb, (((1,), (1,)), ((), ())), preferred_element_type=F32)
        vt_ref[...] = vt.astype(BF16)
    for n, o_ref in enumerate(out_refs):
        for hp in range(d // LANES):
            c0 = n * d + hp * LANES
            o_ref[hp] = res[:, c0:c0 + LANES].astype(BF16)


def _sb_proj(h, w, wvt=None):
    bsz, seq, d = h.shape
    n_out = w.shape[1] // d
    tm = PROJ_TOKENS
    hp = d // LANES
    out_sds = [jax.ShapeDtypeStruct((bsz, hp, seq, LANES), BF16)] * n_out
    out_specs = [pl.BlockSpec((None, hp, tm, LANES), lambda b, s: (b, 0, s, 0))] * n_out
    in_specs = [pl.BlockSpec((None, tm, d), lambda b, s: (b, s, 0)), _const_spec((d, n_out * d))]
    args = [h, w]
    if wvt is not None:
        in_specs.append(_const_spec((d, d)))
        args.append(wvt)
        out_sds.append(jax.ShapeDtypeStruct((bsz, d, seq), BF16))
        out_specs.append(pl.BlockSpec((None, d, tm), lambda b, s: (b, 0, s)))
    return pl.pallas_call(
        functools.partial(_sb_proj_kernel, with_vt=wvt is not None),
        grid=(bsz, seq // tm),
        in_specs=in_specs,
        out_specs=out_specs,
        out_shape=out_sds,
        compiler_params=pltpu.CompilerParams(
            dimension_semantics=("arbitrary", "arbitrary"), vmem_limit_bytes=VMEM_LIMIT),
        name="sb_proj",
    )(*args)


def _sb_logits(ks, q_blk, bias):
    z = lax.dot_general(ks, q_blk, (((1,), (1,)), ((), ())), preferred_element_type=F32)

    def softplus(x):
        return jnp.maximum(x, jnp.log(1.0 + jnp.exp(jnp.minimum(x, SB_EXP_CLAMP)))).astype(BF16)

    if bias is None:
        return z, softplus(z)
    half = SB_T // 2
    zs, ts = [], []
    for hd in range(2):
        r0 = hd * SB_T
        za = z[r0:r0 + half, :half] + bias[r0:r0 + half, :half]
        zb = z[r0:r0 + SB_T, half:] + bias[r0:r0 + SB_T, half:]
        ta = jnp.concatenate([softplus(za), jnp.zeros((half, half), BF16)], axis=0)
        zs.append((za, zb))
        ts.append(jnp.concatenate([ta, softplus(zb)], axis=1))
    return zs, jnp.concatenate(ts, axis=0)


def _sb_weights(z, tb, aa_ref, r):
    half = SB_T // 2
    ws, rs = [], []
    for hd in range(2):
        rows = slice(hd * SB_T, (hd + 1) * SB_T)
        cs = jnp.dot(aa_ref[...], tb[rows], preferred_element_type=F32)
        if r is None:
            za, zb = z[hd]
            wa = jnp.exp(za - cs[:half, :half]).astype(BF16)
            wb = jnp.exp(zb - cs[:SB_T, half:]).astype(BF16)
            wa = jnp.concatenate([wa, jnp.zeros((half, half), BF16)], axis=0)
            ws.append(jnp.concatenate([wa, wb], axis=1))
        else:
            ws.append(jnp.exp(z[rows] - cs[:SB_T] - r[hd]).astype(BF16))
        rs.append(cs[SB_T:SB_T + 1])
    return jnp.concatenate(ws, axis=0), rs


def _sb_values(vt2, w):
    return jnp.dot(vt2, w, preferred_element_type=F32)


def _sb_tile(ks, q_blk, vt2, aa_ref, bias, r):
    z, tb = _sb_logits(ks, q_blk, bias)
    w, rs = _sb_weights(z, tb, aa_ref, r)
    return _sb_values(vt2, w), rs


def _sb_attn_kernel(q_ref, k_ref, vt_ref, aa_ref, bias_ref, o_ref, acc_ref, r_ref):
    n_blocks = q_ref.shape[0] // SB_T
    lane_k = lax.broadcasted_iota(jnp.int32, (SB_T, LANES), 1)
    row_v = lax.broadcasted_iota(jnp.int32, (LANES, SB_T), 0)

    def key_tile(k0):
        k = k_ref[pl.ds(k0, SB_T), :]
        vt = vt_ref[:, pl.ds(k0, SB_T)]
        zk, zv = jnp.zeros_like(k), jnp.zeros_like(vt)
        ks = jnp.concatenate([jnp.where(lane_k < SB_HEAD_DIM, k, zk),
                              jnp.where(lane_k >= SB_HEAD_DIM, k, zk)], axis=0)
        vt2 = jnp.concatenate([jnp.where(row_v < SB_HEAD_DIM, vt, zv),
                               jnp.where(row_v >= SB_HEAD_DIM, vt, zv)], axis=1)
        return ks, vt2

    def tile(q_blk, kv, bias, r):
        return _sb_tile(kv[0], q_blk, kv[1], aa_ref, bias, r)

    kvs = [key_tile(m * SB_T) for m in range(n_blocks)]
    tiles = [(m, j) for m in range(n_blocks) for j in ((m, m - 1) if m else (m,))]
    logits, weights, rsums, accs = {}, {}, {}, {}
    lowest_sums = None
    for i in range(len(tiles) + 2):
        if i < len(tiles):
            m, j = tiles[i]
            logits[i] = _sb_logits(kvs[j][0], q_ref[m * SB_T:(m + 1) * SB_T, :],
                                   bias_ref[...] if j == m else None)
        if 0 <= i - 1 < len(tiles):
            m, j = tiles[i - 1]
            weights[i - 1], rs = _sb_weights(*logits.pop(i - 1), aa_ref, rsums.get(m))
            rsums[m] = [a + b for a, b in zip(rsums[m], rs)] if m in rsums else rs
        if 0 <= i - 2 < len(tiles):
            m, j = tiles[i - 2]
            pv = _sb_values(kvs[j][1], weights.pop(i - 2))
            accs[m] = accs[m] + pv if m in accs else pv
            if j == max(m - 1, 0):
                o_ref[:, m * SB_T:(m + 1) * SB_T] = accs[m].astype(o_ref.dtype)
                if m > 1:
                    acc_ref[m] = accs[m]
                    r_ref[m] = jnp.concatenate(rsums[m], axis=0)
                    lowest_sums = rsums[m] if lowest_sums is None else [
                        jnp.minimum(a, b) for a, b in zip(lowest_sums, rsums[m])]

    def continue_block(m, carry):
        q0 = pl.multiple_of(m * SB_T, SB_T)

        def more(c):
            j, _, _, lowest = c
            return jnp.logical_and(j >= 0, lowest < SB_LOG_WEIGHT_FLOOR)

        def step(c):
            j, acc, rsum, _ = c
            pv, rs = tile(q_ref[pl.ds(q0, SB_T), :], key_tile(pl.multiple_of(j * SB_T, SB_T)),
                          None, [rsum[0:1], rsum[1:2]])
            rsum = rsum + jnp.concatenate(rs, axis=0)
            return j - 1, acc + pv, rsum, jnp.min(rsum)

        rsum = r_ref[m]
        lowest = jnp.min(rsum)

        @pl.when(lowest < SB_LOG_WEIGHT_FLOOR)
        def _():
            _, acc, _, _ = lax.while_loop(more, step, (m - 2, acc_ref[m], rsum, lowest))
            o_ref[:, pl.ds(q0, SB_T)] = acc.astype(o_ref.dtype)

        return carry

    if lowest_sums is not None:
        @pl.when(jnp.min(jnp.minimum(*lowest_sums)) < SB_LOG_WEIGHT_FLOOR)
        def _():
            lax.fori_loop(2, n_blocks, continue_block, 0)


def _sb_attention(q, k, vt):
    bsz, hp, seq, _ = q.shape
    ar = jnp.arange(SB_T)
    later = (ar[None, :] >= ar[:, None]).astype(BF16)
    aa = jnp.concatenate([later, jnp.ones((SB_SUM_ROWS, SB_T), BF16)], axis=0)
    causal = ar[:, None] < ar[None, :]
    bias = jnp.where(jnp.concatenate([causal, causal], axis=0), 0.0, SB_MASK_VALUE).astype(F32)
    slab = pl.BlockSpec((None, None, seq, LANES), lambda b, p: (b, p, 0, 0))
    slab_t = pl.BlockSpec((None, LANES, seq), lambda b, p: (b, p, 0))
    return pl.pallas_call(
        _sb_attn_kernel,
        grid=(bsz, hp),
        in_specs=[slab, slab, slab_t, _const_spec((SB_T + SB_SUM_ROWS, SB_T)),
                  _const_spec((2 * SB_T, SB_T))],
        out_specs=slab_t,
        out_shape=jax.ShapeDtypeStruct((bsz, hp * LANES, seq), BF16),
        scratch_shapes=[
            pltpu.VMEM((seq // SB_T, LANES, SB_T), F32),
            pltpu.VMEM((seq // SB_T, 2, SB_T), F32),
        ],
        compiler_params=pltpu.CompilerParams(
            dimension_semantics=("arbitrary", "arbitrary"), vmem_limit_bytes=VMEM_LIMIT),
        name="sb_attention",
    )(q, k, vt, aa, bias)


def kernel(x, gla_w_in, gla_w_a2, gla_b_a, gla_norm_g, gla_w_o, kv_w, sb_w_q, sb_w_o,
           mlp_w_up, mlp_w_down, ln_g, ln_b):
    bsz, seq, d = x.shape
    depth = mlp_w_up.shape[0]
    n_a = gla_w_in.shape[0]
    alpha = (2.0 * depth) ** 0.25
    q_scale = SB_HEAD_DIM ** -0.5

    h = x
    q = k_sh = vt_sh = None
    for layer in range(depth):
        mixer = None
        if layer < n_a:
            h = _gla_layer(h, gla_w_in[layer], gla_w_a2[layer], gla_b_a[layer], gla_norm_g[layer],
                           gla_w_o[layer], ln_g[layer, 0], ln_b[layer, 0], alpha)
        else:
            j = layer - n_a
            if j == 0:
                wq = (sb_w_q[j] * q_scale).astype(BF16)
                wqk = jnp.concatenate([wq, kv_w[:, :d].astype(BF16)], axis=1)
                q, k_sh, vt_sh = _sb_proj(h, wqk, kv_w[:, d:].T.astype(BF16))
            mixer = (_sb_attention(q, k_sh, vt_sh), sb_w_o[j], ln_g[layer, 0], ln_b[layer, 0])
        next_wq = None
        if n_a <= layer < depth - 1:
            next_wq = (sb_w_q[layer + 1 - n_a] * q_scale).astype(BF16)
        outs = _mlp_layer(h, mlp_w_up[layer], mlp_w_down[layer], ln_g[layer, 1], ln_b[layer, 1],
                          alpha, mixer=mixer, wq=next_wq)
        h = outs[0]
        if next_wq is not None:
            q = outs[1]
    return h
```

```python
import functools

import jax
import jax.numpy as jnp
from jax import lax
from jax.experimental import pallas as pl
from jax.experimental.pallas import tpu as pltpu

F32 = jnp.float32
BF16 = jnp.bfloat16

GLA_HEADS = 4
GATE_RANK = 16
GATE_TAU = 16.0
GLA_CHUNK = 64
SB_HEAD_DIM = 64
LN_EPS = 1e-5
RMS_EPS = 1e-6

LANES = 128
VMEM_LIMIT = 56 * 1024 * 1024

GLA_TOKENS = 512
GLA_PROJ_PIECE = 256
MLP_TOKENS = 512
MLP_ANCHOR_CHUNKS = (1, 2)
MLP_FF_CHUNK = 1024
PROJ_TOKENS = 1024
SB_T = 256
SB_SUM_ROWS = 16
SB_EXP_CLAMP = 88.0
SB_MASK_VALUE = -1e30
SB_LOG_WEIGHT_FLOOR = 88.0


def _const_spec(shape):
    nd = len(shape)
    return pl.BlockSpec(shape, lambda *_: (0,) * nd, pipeline_mode=pl.Buffered(1))


def _layer_norm(y, g, b):
    mu = jnp.mean(y, axis=-1, keepdims=True)
    yc = y - mu
    var = jnp.mean(yc * yc, axis=-1, keepdims=True)
    return yc * lax.rsqrt(var + LN_EPS) * g + b


def _log_sigmoid(x):
    return jnp.minimum(x, 0.0) - jnp.log(1.0 + jnp.exp(-jnp.abs(x)))


def _split3_bf16(x):
    hi = x.astype(BF16)
    r1 = x - hi.astype(F32)
    mid = r1.astype(BF16)
    lo = (r1 - mid.astype(F32)).astype(BF16)
    return hi, mid, lo


def _gla_kernel(h_ref, hprev_ref, wqkvg_ref, wa1_ref, wa2_ref, ba_ref, ng_ref, wo_ref, lng_ref,
                lnb_ref, tri_ref, o_ref, state_ref, og_ref, proj0_ref, proj1_ref, a10_ref, a11_ref,
                hb_ref, *, alpha, dqk, dv, tiles_per_seq):
    g = pl.program_id(0)
    stash = ((proj0_ref, a10_ref), (proj1_ref, a11_ref))

    @pl.when(g == 0)
    def _():
        for ref in stash[1]:
            ref[...] = jnp.zeros_like(ref)

    @pl.when(lax.rem(g + tiles_per_seq - 1, tiles_per_seq) == 0)
    def _():
        state_ref[...] = jnp.zeros_like(state_ref)

    def step(cur, prev):
        pieces = _gla_projection_pieces(h_ref, wqkvg_ref, wa1_ref, hb_ref, *stash[cur])
        _gla_mix(*stash[prev], hprev_ref, wa2_ref, ba_ref, ng_ref, wo_ref, lng_ref, lnb_ref, tri_ref,
                 o_ref, state_ref, og_ref, alpha=alpha, dqk=dqk, dv=dv,
                 between_units=lambda: next(pieces, None))
        for _ in pieces:
            pass

    @pl.when(lax.rem(g, 2) == 0)
    def _():
        step(0, 1)

    @pl.when(lax.rem(g, 2) == 1)
    def _():
        step(1, 0)


def _gla_projection_pieces(h_ref, wqkvg_ref, wa1_ref, hb_ref, proj_ref, a1_ref):
    hb_ref[...] = h_ref[...].astype(BF16)
    yield
    a1_ref[...] = jnp.dot(hb_ref[...], wa1_ref[...], preferred_element_type=F32)
    yield
    for c0 in range(0, proj_ref.shape[1], GLA_PROJ_PIECE):
        cols = slice(c0, c0 + GLA_PROJ_PIECE)
        proj_ref[:, cols] = jnp.dot(hb_ref[...], wqkvg_ref[:, cols], preferred_element_type=F32)
        yield


def _gla_mix(proj_ref, a1_ref, hres_ref, wa2_ref, ba_ref, ng_ref, wo_ref, lng_ref, lnb_ref,
             tri_ref, o_ref, state_ref, og_ref, *, alpha, dqk, dv, between_units):
    hk = dqk // GLA_HEADS
    hv = dv // GLA_HEADS
    n_chunks = proj_ref.shape[0] // GLA_CHUNK

    gate = jnp.dot(a1_ref[...].astype(BF16), wa2_ref[...], preferred_element_type=F32) + ba_ref[...]
    for _ in range(4):
        between_units()
    log_a = _log_sigmoid(gate) * (1.0 / GATE_TAU)

    tri3 = tri_ref[...]
    row = lax.broadcasted_iota(jnp.int32, (GLA_CHUNK, GLA_CHUNK), 0)
    col = lax.broadcasted_iota(jnp.int32, (GLA_CHUNK, GLA_CHUNK), 1)
    causal = row >= col
    q_scale = hk ** -0.5

    b_chunks = []
    for c in range(n_chunks):
        hi, mid, lo = _split3_bf16(log_a[c * GLA_CHUNK:(c + 1) * GLA_CHUNK, :])
        b_chunks.append(jnp.dot(tri3, jnp.concatenate([hi, mid, lo], axis=0),
                                preferred_element_type=F32))

    def scores(c, hd):
        rows = slice(c * GLA_CHUNK, (c + 1) * GLA_CHUNK)
        q = proj_ref[rows, hd * hk:(hd + 1) * hk]
        k = proj_ref[rows, dqk + hd * hk:dqk + (hd + 1) * hk]
        b = b_chunks[c][:, hd * hk:(hd + 1) * hk]
        b_last = b[GLA_CHUNK - 1:GLA_CHUNK, :]
        q_dec = ((q * q_scale) * jnp.exp(b)).astype(BF16)
        k_dec = (k * jnp.exp(-b)).astype(BF16)
        k_end = (k * jnp.exp(b_last - b)).astype(BF16)
        att = lax.dot_general(q_dec, k_dec, (((1,), (1,)), ((), ())),
                              preferred_element_type=F32)
        return q_dec, k_end, b_last, att

    def finish(c, hd, q_dec, k_end, b_last, att):
        rows = slice(c * GLA_CHUNK, (c + 1) * GLA_CHUNK)
        v = proj_ref[rows, 2 * dqk + hd * hv:2 * dqk + (hd + 1) * hv]
        g = proj_ref[rows, 2 * dqk + dv + hd * hv:2 * dqk + dv + (hd + 1) * hv]
        vb = v.astype(BF16)
        att = jnp.where(causal, att, 0.0).astype(BF16)
        st = state_ref[hd]
        o = jnp.dot(jnp.concatenate([q_dec, att], axis=1),
                    jnp.concatenate([st.astype(BF16), vb], axis=0),
                    preferred_element_type=F32)
        u = lax.dot_general(k_end, vb, (((0,), (0,)), ((), ())),
                            preferred_element_type=F32)
        decay = jnp.exp(jnp.broadcast_to(b_last, (hk, hk)).T)
        state_ref[hd] = st * jnp.concatenate([decay] * (hv // hk), axis=1) + u
        o = o * lax.rsqrt(jnp.mean(o * o, axis=-1, keepdims=True) + RMS_EPS)
        o = o * ng_ref[...]
        o = o * (g / (1.0 + jnp.exp(-g)))
        og_ref[rows, hd * hv:(hd + 1) * hv] = o.astype(BF16)

    units = [(c, hd) for c in range(n_chunks) for hd in range(GLA_HEADS)]
    pending = scores(*units[0])
    for i, unit in enumerate(units):
        ahead = scores(*units[i + 1]) if i + 1 < len(units) else None
        if i % (len(units) // 8) == 0:
            between_units()
        finish(*unit, *pending)
        pending = ahead

    mix = jnp.dot(og_ref[...], wo_ref[...], preferred_element_type=F32)
    for _ in range(3):
        between_units()
    o_ref[...] = _layer_norm(alpha * hres_ref[...] + mix, lng_ref[...], lnb_ref[...])


def _gla_layer(h, w_in, w_a2, b_a, norm_g, w_o, ln_g, ln_b, alpha):
    bsz, seq, d = h.shape
    dqk = w_a2.shape[1]
    dv = w_o.shape[0]
    n_main = 2 * dqk + 2 * dv
    hv = dv // GLA_HEADS
    hk = dqk // GLA_HEADS
    wqkvg = w_in.astype(BF16)
    wa1 = jnp.pad(w_in[:, n_main:], ((0, 0), (0, LANES - GATE_RANK))).astype(BF16)
    wa2 = jnp.pad(w_a2, ((0, LANES - GATE_RANK), (0, 0))).astype(BF16)
    tri = (jnp.arange(GLA_CHUNK)[:, None] >= jnp.arange(GLA_CHUNK)[None, :]).astype(BF16)
    tri3 = jnp.concatenate([tri, tri, tri], axis=1)
    ts = GLA_TOKENS
    assert seq % ts == 0 and n_main % GLA_PROJ_PIECE == 0 and hv % hk == 0
    n_tiles = bsz * seq // ts
    kern = functools.partial(_gla_kernel, alpha=alpha, dqk=dqk, dv=dv, tiles_per_seq=seq // ts)
    out = pl.pallas_call(
        kern,
        grid=(n_tiles + 1,),
        in_specs=[
            pl.BlockSpec((ts, d), lambda g: (jnp.minimum(g, n_tiles - 1), 0)),
            pl.BlockSpec((ts, d), lambda g: (jnp.maximum(g - 1, 0), 0)),
            _const_spec(w_in.shape),
            _const_spec((d, LANES)),
            _const_spec((LANES, dqk)),
            _const_spec((1, dqk)),
            _const_spec((1, hv)),
            _const_spec((dv, d)),
            _const_spec((1, d)),
            _const_spec((1, d)),
            _const_spec((GLA_CHUNK, 3 * GLA_CHUNK)),
        ],
        out_specs=pl.BlockSpec((ts, d), lambda g: (jnp.maximum(g - 1, 0), 0)),
        out_shape=jax.ShapeDtypeStruct((bsz * seq, d), F32),
        scratch_shapes=[
            pltpu.VMEM((GLA_HEADS, hk, hv), F32),
            pltpu.VMEM((ts, dv), BF16),
            pltpu.VMEM((ts, n_main), F32), pltpu.VMEM((ts, n_main), F32),
            pltpu.VMEM((ts, LANES), F32), pltpu.VMEM((ts, LANES), F32),
            pltpu.VMEM((ts, d), BF16),
        ],
        compiler_params=pltpu.CompilerParams(
            dimension_semantics=("arbitrary",), vmem_limit_bytes=VMEM_LIMIT),
        name="gla_layer",
    )(h.reshape(bsz * seq, d), h.reshape(bsz * seq, d), wqkvg, wa1, wa2, b_a.reshape(1, dqk), norm_g.reshape(1, hv),
      w_o.astype(BF16), ln_g.reshape(1, d), ln_b.reshape(1, d), tri3)
    return out.reshape(bsz, seq, d)


def _mlp_kernel(*refs, alpha, with_mixer, with_q, n_tiles):
    refs = list(refs)
    h_ref = refs.pop(0)
    if with_mixer:
        at_ref, wo_ref, g1_ref, b1_ref = (refs.pop(0) for _ in range(4))
    wup_ref, wdn_ref, g_ref, b_ref = (refs.pop(0) for _ in range(4))
    wq_ref = refs.pop(0) if with_q else None
    o_ref = refs.pop(0)
    q_ref = refs.pop(0) if with_q else None
    pre_refs = (refs.pop(0), refs.pop(0))
    if with_mixer:
        h1_refs = (refs.pop(0), refs.pop(0))
        hb_refs = (refs.pop(0), refs.pop(0))
    g = pl.program_id(0)

    def mlp(hb, anchors):
        acc = None
        for i, f in enumerate(range(0, wup_ref.shape[1], MLP_FF_CHUNK)):
            u = jnp.dot(hb, wup_ref[:, f:f + MLP_FF_CHUNK], preferred_element_type=F32)
            u = jnp.maximum(u, 0.0)
            u = (u * u).astype(BF16)
            if i in anchors:
                u = jnp.where(g >= 0, u, anchors[i])
            dn = jnp.dot(u, wdn_ref[f:f + MLP_FF_CHUNK, :], preferred_element_type=F32)
            acc = dn if acc is None else acc + dn
        return acc

    def step(p, run_a, run_b, run_c):
        anchors = {}
        if with_mixer:
            pre_in, pre_out = pre_refs[p], pre_refs[1 - p]
        else:
            pre_in, pre_out = pre_refs[1 - p], pre_refs[p]
        if run_a:
            mix = lax.dot_general(at_ref[...], wo_ref[...], (((0,), (0,)), ((), ())),
                                  preferred_element_type=F32)
        if run_c:
            out = _layer_norm(pre_in[...], g_ref[...], b_ref[...])
            o_ref[...] = out
            out_b = out.astype(BF16)
            anchors[MLP_ANCHOR_CHUNKS[0]] = out_b
        if run_a:
            h1 = _layer_norm(alpha * h_ref[...] + mix, g1_ref[...], b1_ref[...])
            h1_b = h1.astype(BF16)
            h1_refs[p][...] = h1
            hb_refs[p][...] = h1_b
            anchors[MLP_ANCHOR_CHUNKS[1]] = h1_b
        if run_b and with_mixer:
            pre_out[...] = alpha * h1_refs[1 - p][...] + mlp(hb_refs[1 - p][...], anchors)
        elif run_b:
            h = h_ref[...]
            pre_out[...] = alpha * h + mlp(h.astype(BF16), anchors)
        if run_c and with_q:
            q = jnp.dot(out_b, wq_ref[...], preferred_element_type=F32)
            for hp in range(q_ref.shape[0]):
                q_ref[hp] = q[:, hp * LANES:(hp + 1) * LANES].astype(BF16)

    first_full = 2 if with_mixer else 1
    for p in range(2):
        @pl.when(jnp.logical_and(jnp.logical_and(g >= first_full, g < n_tiles),
                                 lax.rem(g, 2) == p))
        def _(p=p):
            step(p, with_mixer, True, True)

    if with_mixer:
        edge_steps = {0: (True, False, False), 1: (True, True, False),
                      n_tiles: (False, True, True), n_tiles + 1: (False, False, True)}
    else:
        edge_steps = {0: (False, True, False), n_tiles: (False, False, True)}
    for g_edge, stages in edge_steps.items():
        @pl.when(g == g_edge)
        def _(g_edge=g_edge, stages=stages):
            step(g_edge % 2, *stages)


def _mlp_layer(h, w_up, w_down, ln_g, ln_b, alpha, mixer=None, wq=None):
    bsz, seq, d = h.shape
    dff = w_up.shape[1]
    tm = MLP_TOKENS
    assert seq % tm == 0 and dff % MLP_FF_CHUNK == 0 and d == MLP_FF_CHUNK
    tiles_per_seq = seq // tm
    n_tiles = bsz * tiles_per_seq
    lag = 2 if mixer is not None else 1
    assert n_tiles > lag, "the staged MLP kernel needs more token tiles than stages"
    row = lambda v: v.reshape(1, d)
    tile_in = lambda g: jnp.minimum(g, n_tiles - 1)
    tile_out = lambda g: jnp.maximum(g - lag, 0)
    in_specs, args = [pl.BlockSpec((tm, d), lambda g: (tile_in(g), 0))], [h.reshape(bsz * seq, d)]
    if mixer is not None:
        at, w_o, g1, b1 = mixer
        in_specs += [pl.BlockSpec((None, d, tm), lambda g: (tile_in(g) // tiles_per_seq, 0,
                                                            tile_in(g) % tiles_per_seq)),
                     _const_spec((d, d)), _const_spec((1, d)), _const_spec((1, d))]
        args += [at, w_o.astype(BF16), row(g1), row(b1)]
    in_specs += [_const_spec((d, dff)), _const_spec((dff, d)), _const_spec((1, d)), _const_spec((1, d))]
    args += [w_up.astype(BF16), w_down.astype(BF16), row(ln_g), row(ln_b)]
    out_specs = [pl.BlockSpec((tm, d), lambda g: (tile_out(g), 0))]
    out_shape = [jax.ShapeDtypeStruct((bsz * seq, d), F32)]
    if wq is not None:
        in_specs.append(_const_spec((d, d)))
        args.append(wq)
        out_specs.append(pl.BlockSpec((None, d // LANES, tm, LANES),
                                      lambda g: (tile_out(g) // tiles_per_seq, 0,
                                                 tile_out(g) % tiles_per_seq, 0)))
        out_shape.append(jax.ShapeDtypeStruct((bsz, d // LANES, seq, LANES), BF16))
    scratch = [pltpu.VMEM((tm, d), F32)] * 2
    if mixer is not None:
        scratch += [pltpu.VMEM((tm, d), F32)] * 2
        scratch += [pltpu.VMEM((tm, d), BF16)] * 2
    outs = pl.pallas_call(
        functools.partial(_mlp_kernel, alpha=alpha, with_mixer=mixer is not None,
                          with_q=wq is not None, n_tiles=n_tiles),
        grid=(n_tiles + lag,),
        in_specs=in_specs,
        out_specs=out_specs,
        out_shape=out_shape,
        scratch_shapes=scratch,
        compiler_params=pltpu.CompilerParams(
            dimension_semantics=("arbitrary",), vmem_limit_bytes=VMEM_LIMIT),
        name="mlp_layer",
    )(*args)
    return [outs[0].reshape(bsz, seq, d)] + list(outs[1:])


def _sb_proj_kernel(h_ref, w_ref, *refs, with_vt):
    hb = h_ref[...].astype(BF16)
    d = h_ref.shape[1]
    res = jnp.dot(hb, w_ref[...], preferred_element_type=F32)
    out_refs = refs
    if with_vt:
        wvt_ref, out_refs, vt_ref = refs[0], refs[1:-1], refs[-1]
        vt = lax.dot_general(wvt_ref[...], hb, (((1,), (1,)), ((), ())), preferred_element_type=F32)
        vt_ref[...] = vt.astype(BF16)
    for n, o_ref in enumerate(out_refs):
        for hp in range(d // LANES):
            c0 = n * d + hp * LANES
            o_ref[hp] = res[:, c0:c0 + LANES].astype(BF16)


def _sb_proj(h, w, wvt=None):
    bsz, seq, d = h.shape
    n_out = w.shape[1] // d
    tm = PROJ_TOKENS
    assert seq % tm == 0 and d % LANES == 0
    hp = d // LANES
    out_sds = [jax.ShapeDtypeStruct((bsz, hp, seq, LANES), BF16)] * n_out
    out_specs = [pl.BlockSpec((None, hp, tm, LANES), lambda b, s: (b, 0, s, 0))] * n_out
    in_specs = [pl.BlockSpec((None, tm, d), lambda b, s: (b, s, 0)), _const_spec((d, n_out * d))]
    args = [h, w]
    if wvt is not None:
        in_specs.append(_const_spec((d, d)))
        args.append(wvt)
        out_sds.append(jax.ShapeDtypeStruct((bsz, d, seq), BF16))
        out_specs.append(pl.BlockSpec((None, d, tm), lambda b, s: (b, 0, s)))
    return pl.pallas_call(
        functools.partial(_sb_proj_kernel, with_vt=wvt is not None),
        grid=(bsz, seq // tm),
        in_specs=in_specs,
        out_specs=out_specs,
        out_shape=out_sds,
        compiler_params=pltpu.CompilerParams(
            dimension_semantics=("arbitrary", "arbitrary"), vmem_limit_bytes=VMEM_LIMIT),
        name="sb_proj",
    )(*args)


def _sb_logits(ks, q_blk, bias):
    z = lax.dot_general(ks, q_blk, (((1,), (1,)), ((), ())), preferred_element_type=F32)

    def softplus(x):
        return jnp.maximum(x, jnp.log(1.0 + jnp.exp(jnp.minimum(x, SB_EXP_CLAMP)))).astype(BF16)

    if bias is None:
        return z, softplus(z)
    half = SB_T // 2
    zs, ts = [], []
    for hd in range(2):
        r0 = hd * SB_T
        za = z[r0:r0 + half, :half] + bias[r0:r0 + half, :half]
        zb = z[r0:r0 + SB_T, half:] + bias[r0:r0 + SB_T, half:]
        ta = jnp.concatenate([softplus(za), jnp.zeros((half, half), BF16)], axis=0)
        zs.append((za, zb))
        ts.append(jnp.concatenate([ta, softplus(zb)], axis=1))
    return zs, jnp.concatenate(ts, axis=0)


def _sb_weights(z, tb, aa_ref, r):
    half = SB_T // 2
    ws, rs = [], []
    for hd in range(2):
        rows = slice(hd * SB_T, (hd + 1) * SB_T)
        cs = jnp.dot(aa_ref[...], tb[rows], preferred_element_type=F32)
        if r is None:
            za, zb = z[hd]
            wa = jnp.exp(za - cs[:half, :half]).astype(BF16)
            wb = jnp.exp(zb - cs[:SB_T, half:]).astype(BF16)
            wa = jnp.concatenate([wa, jnp.zeros((half, half), BF16)], axis=0)
            ws.append(jnp.concatenate([wa, wb], axis=1))
        else:
            ws.append(jnp.exp(z[rows] - cs[:SB_T] - r[hd]).astype(BF16))
        rs.append(cs[SB_T:SB_T + 1])
    return jnp.concatenate(ws, axis=0), rs


def _sb_values(vt2, w):
    return jnp.dot(vt2, w, preferred_element_type=F32)


def _sb_tile(ks, q_blk, vt2, aa_ref, bias, r):
    z, tb = _sb_logits(ks, q_blk, bias)
    w, rs = _sb_weights(z, tb, aa_ref, r)
    return _sb_values(vt2, w), rs


def _sb_attn_kernel(q_ref, k_ref, vt_ref, aa_ref, bias_ref, o_ref, acc_ref, r_ref):
    n_blocks = q_ref.shape[0] // SB_T
    lane_k = lax.broadcasted_iota(jnp.int32, (SB_T, LANES), 1)
    row_v = lax.broadcasted_iota(jnp.int32, (LANES, SB_T), 0)

    def key_tile(k0):
        k = k_ref[pl.ds(k0, SB_T), :]
        vt = vt_ref[:, pl.ds(k0, SB_T)]
        zk, zv = jnp.zeros_like(k), jnp.zeros_like(vt)
        ks = jnp.concatenate([jnp.where(lane_k < SB_HEAD_DIM, k, zk),
                              jnp.where(lane_k >= SB_HEAD_DIM, k, zk)], axis=0)
        vt2 = jnp.concatenate([jnp.where(row_v < SB_HEAD_DIM, vt, zv),
                               jnp.where(row_v >= SB_HEAD_DIM, vt, zv)], axis=1)
        return ks, vt2

    def tile(q_blk, kv, bias, r):
        return _sb_tile(kv[0], q_blk, kv[1], aa_ref, bias, r)

    kvs = [key_tile(m * SB_T) for m in range(n_blocks)]
    tiles = [(m, j) for m in range(n_blocks) for j in ((m, m - 1) if m else (m,))]
    logits, weights, rsums, accs = {}, {}, {}, {}
    lowest_sums = None
    for i in range(len(tiles) + 2):
        if i < len(tiles):
            m, j = tiles[i]
            logits[i] = _sb_logits(kvs[j][0], q_ref[m * SB_T:(m + 1) * SB_T, :],
                                   bias_ref[...] if j == m else None)
        if 0 <= i - 1 < len(tiles):
            m, j = tiles[i - 1]
            weights[i - 1], rs = _sb_weights(*logits.pop(i - 1), aa_ref, rsums.get(m))
            rsums[m] = [a + b for a, b in zip(rsums[m], rs)] if m in rsums else rs
        if 0 <= i - 2 < len(tiles):
            m, j = tiles[i - 2]
            pv = _sb_values(kvs[j][1], weights.pop(i - 2))
            accs[m] = accs[m] + pv if m in accs else pv
            if j == max(m - 1, 0):
                o_ref[:, m * SB_T:(m + 1) * SB_T] = accs[m].astype(o_ref.dtype)
                if m > 1:
                    acc_ref[m] = accs[m]
                    r_ref[m] = jnp.concatenate(rsums[m], axis=0)
                    lowest_sums = rsums[m] if lowest_sums is None else [
                        jnp.minimum(a, b) for a, b in zip(lowest_sums, rsums[m])]

    def continue_block(m, carry):
        q0 = pl.multiple_of(m * SB_T, SB_T)

        def more(c):
            j, _, _, lowest = c
            return jnp.logical_and(j >= 0, lowest < SB_LOG_WEIGHT_FLOOR)

        def step(c):
            j, acc, rsum, _ = c
            pv, rs = tile(q_ref[pl.ds(q0, SB_T), :], key_tile(pl.multiple_of(j * SB_T, SB_T)),
                          None, [rsum[0:1], rsum[1:2]])
            rsum = rsum + jnp.concatenate(rs, axis=0)
            return j - 1, acc + pv, rsum, jnp.min(rsum)

        rsum = r_ref[m]
        lowest = jnp.min(rsum)

        @pl.when(lowest < SB_LOG_WEIGHT_FLOOR)
        def _():
            _, acc, _, _ = lax.while_loop(more, step, (m - 2, acc_ref[m], rsum, lowest))
            o_ref[:, pl.ds(q0, SB_T)] = acc.astype(o_ref.dtype)

        return carry

    if lowest_sums is not None:
        @pl.when(jnp.min(jnp.minimum(*lowest_sums)) < SB_LOG_WEIGHT_FLOOR)
        def _():
            lax.fori_loop(2, n_blocks, continue_block, 0)


def _sb_attention(q, k, vt):
    bsz, hp, seq, _ = q.shape
    assert seq % SB_T == 0
    ar = jnp.arange(SB_T)
    later = (ar[None, :] >= ar[:, None]).astype(BF16)
    aa = jnp.concatenate([later, jnp.ones((SB_SUM_ROWS, SB_T), BF16)], axis=0)
    causal = ar[:, None] < ar[None, :]
    bias = jnp.where(jnp.concatenate([causal, causal], axis=0), 0.0, SB_MASK_VALUE).astype(F32)
    slab = pl.BlockSpec((None, None, seq, LANES), lambda b, p: (b, p, 0, 0))
    slab_t = pl.BlockSpec((None, LANES, seq), lambda b, p: (b, p, 0))
    return pl.pallas_call(
        _sb_attn_kernel,
        grid=(bsz, hp),
        in_specs=[slab, slab, slab_t, _const_spec((SB_T + SB_SUM_ROWS, SB_T)),
                  _const_spec((2 * SB_T, SB_T))],
        out_specs=slab_t,
        out_shape=jax.ShapeDtypeStruct((bsz, hp * LANES, seq), BF16),
        scratch_shapes=[
            pltpu.VMEM((seq // SB_T, LANES, SB_T), F32),
            pltpu.VMEM((seq // SB_T, 2, SB_T), F32),
        ],
        compiler_params=pltpu.CompilerParams(
            dimension_semantics=("arbitrary", "arbitrary"), vmem_limit_bytes=VMEM_LIMIT),
        name="sb_attention",
    )(q, k, vt, aa, bias)


def kernel(x, gla_w_in, gla_w_a2, gla_b_a, gla_norm_g, gla_w_o, kv_w, sb_w_q, sb_w_o,
           mlp_w_up, mlp_w_down, ln_g, ln_b):
    bsz, seq, d = x.shape
    depth = mlp_w_up.shape[0]
    n_a = gla_w_in.shape[0]
    alpha = (2.0 * depth) ** 0.25
    q_scale = SB_HEAD_DIM ** -0.5

    h = x
    q = k_sh = vt_sh = None
    for layer in range(depth):
        mixer = None
        if layer < n_a:
            h = _gla_layer(h, gla_w_in[layer], gla_w_a2[layer], gla_b_a[layer], gla_norm_g[layer],
                           gla_w_o[layer], ln_g[layer, 0], ln_b[layer, 0], alpha)
        else:
            j = layer - n_a
            if j == 0:
                wq = (sb_w_q[j] * q_scale).astype(BF16)
                wqk = jnp.concatenate([wq, kv_w[:, :d].astype(BF16)], axis=1)
                q, k_sh, vt_sh = _sb_proj(h, wqk, kv_w[:, d:].T.astype(BF16))
            mixer = (_sb_attention(q, k_sh, vt_sh), sb_w_o[j], ln_g[layer, 0], ln_b[layer, 0])
        next_wq = None
        if n_a <= layer < depth - 1:
            next_wq = (sb_w_q[layer + 1 - n_a] * q_scale).astype(BF16)
        outs = _mlp_layer(h, mlp_w_up[layer], mlp_w_down[layer], ln_g[layer, 1], ln_b[layer, 1],
                          alpha, mixer=mixer, wq=next_wq)
        h = outs[0]
        if next_wq is not None:
            q = outs[1]
    return h
```

```python
import functools

import jax
import jax.numpy as jnp
from jax import lax
from jax.experimental import pallas as pl
from jax.experimental.pallas import tpu as pltpu

F32 = jnp.float32
BF16 = jnp.bfloat16

GLA_HEADS = 4
GATE_RANK = 16
GATE_TAU = 16.0
GLA_CHUNK = 64
SB_HEAD_DIM = 64
LN_EPS = 1e-5
RMS_EPS = 1e-6

LANES = 128
VMEM_LIMIT = 56 * 1024 * 1024

GLA_TOKENS = 512
GLA_PROJ_PIECE = 256
MLP_TOKENS = 512
MLP_ANCHOR_CHUNKS = (1, 2)
MLP_FF_CHUNK = 1024
PROJ_TOKENS = 2048
SB_T = 256
SB_SUM_ROWS = 16
SB_EXP_CLAMP = 88.0
SB_MASK_VALUE = -1e30
SB_LOG_WEIGHT_FLOOR = 88.0


def _const_spec(shape):
    nd = len(shape)
    return pl.BlockSpec(shape, lambda *_: (0,) * nd, pipeline_mode=pl.Buffered(1))


def _layer_norm(y, g, b):
    mu = jnp.mean(y, axis=-1, keepdims=True)
    yc = y - mu
    var = jnp.mean(yc * yc, axis=-1, keepdims=True)
    return yc * lax.rsqrt(var + LN_EPS) * g + b


def _log_sigmoid(x):
    return jnp.minimum(x, 0.0) - jnp.log(1.0 + jnp.exp(-jnp.abs(x)))


def _split3_bf16(x):
    hi = x.astype(BF16)
    r1 = x - hi.astype(F32)
    mid = r1.astype(BF16)
    lo = (r1 - mid.astype(F32)).astype(BF16)
    return hi, mid, lo


def _gla_kernel(h_ref, hprev_ref, wqkvg_ref, wa1_ref, wa2_ref, ba_ref, ng_ref, wo_ref, lng_ref,
                lnb_ref, tri_ref, o_ref, state_ref, og_ref, proj0_ref, proj1_ref, a10_ref, a11_ref,
                hb_ref, *, alpha, dqk, dv, tiles_per_seq):
    g = pl.program_id(0)
    stash = ((proj0_ref, a10_ref), (proj1_ref, a11_ref))

    @pl.when(g == 0)
    def _():
        for ref in stash[1]:
            ref[...] = jnp.zeros_like(ref)

    @pl.when(lax.rem(g + tiles_per_seq - 1, tiles_per_seq) == 0)
    def _():
        state_ref[...] = jnp.zeros_like(state_ref)

    def step(cur, prev):
        pieces = _gla_projection_pieces(h_ref, wqkvg_ref, wa1_ref, hb_ref, *stash[cur])
        _gla_mix(*stash[prev], hprev_ref, wa2_ref, ba_ref, ng_ref, wo_ref, lng_ref, lnb_ref, tri_ref,
                 o_ref, state_ref, og_ref, alpha=alpha, dqk=dqk, dv=dv,
                 between_units=lambda: next(pieces, None))
        for _ in pieces:
            pass

    @pl.when(lax.rem(g, 2) == 0)
    def _():
        step(0, 1)

    @pl.when(lax.rem(g, 2) == 1)
    def _():
        step(1, 0)


def _gla_projection_pieces(h_ref, wqkvg_ref, wa1_ref, hb_ref, proj_ref, a1_ref):
    hb_ref[...] = h_ref[...].astype(BF16)
    yield
    a1_ref[...] = jnp.dot(hb_ref[...], wa1_ref[...], preferred_element_type=F32)
    yield
    for c0 in range(0, proj_ref.shape[1], GLA_PROJ_PIECE):
        cols = slice(c0, c0 + GLA_PROJ_PIECE)
        proj_ref[:, cols] = jnp.dot(hb_ref[...], wqkvg_ref[:, cols], preferred_element_type=F32)
        yield


def _gla_mix(proj_ref, a1_ref, hres_ref, wa2_ref, ba_ref, ng_ref, wo_ref, lng_ref, lnb_ref,
             tri_ref, o_ref, state_ref, og_ref, *, alpha, dqk, dv, between_units):
    hk = dqk // GLA_HEADS
    hv = dv // GLA_HEADS
    n_chunks = proj_ref.shape[0] // GLA_CHUNK

    gate = jnp.dot(a1_ref[...].astype(BF16), wa2_ref[...], preferred_element_type=F32) + ba_ref[...]
    for _ in range(4):
        between_units()
    log_a = _log_sigmoid(gate) * (1.0 / GATE_TAU)

    tri3 = tri_ref[...]
    row = lax.broadcasted_iota(jnp.int32, (GLA_CHUNK, GLA_CHUNK), 0)
    col = lax.broadcasted_iota(jnp.int32, (GLA_CHUNK, GLA_CHUNK), 1)
    causal = row >= col
    q_scale = hk ** -0.5

    b_chunks = []
    for c in range(n_chunks):
        hi, mid, lo = _split3_bf16(log_a[c * GLA_CHUNK:(c + 1) * GLA_CHUNK, :])
        b_chunks.append(jnp.dot(tri3, jnp.concatenate([hi, mid, lo], axis=0),
                                preferred_element_type=F32))

    def scores(c, hd):
        rows = slice(c * GLA_CHUNK, (c + 1) * GLA_CHUNK)
        q = proj_ref[rows, hd * hk:(hd + 1) * hk]
        k = proj_ref[rows, dqk + hd * hk:dqk + (hd + 1) * hk]
        b = b_chunks[c][:, hd * hk:(hd + 1) * hk]
        b_last = b[GLA_CHUNK - 1:GLA_CHUNK, :]
        q_dec = ((q * q_scale) * jnp.exp(b)).astype(BF16)
        k_dec = (k * jnp.exp(-b)).astype(BF16)
        k_end = (k * jnp.exp(b_last - b)).astype(BF16)
        att = lax.dot_general(q_dec, k_dec, (((1,), (1,)), ((), ())),
                              preferred_element_type=F32)
        return q_dec, k_end, b_last, att

    def finish(c, hd, q_dec, k_end, b_last, att):
        rows = slice(c * GLA_CHUNK, (c + 1) * GLA_CHUNK)
        v = proj_ref[rows, 2 * dqk + hd * hv:2 * dqk + (hd + 1) * hv]
        g = proj_ref[rows, 2 * dqk + dv + hd * hv:2 * dqk + dv + (hd + 1) * hv]
        vb = v.astype(BF16)
        att = jnp.where(causal, att, 0.0).astype(BF16)
        st = state_ref[hd]
        o = jnp.dot(jnp.concatenate([q_dec, att], axis=1),
                    jnp.concatenate([st.astype(BF16), vb], axis=0),
                    preferred_element_type=F32)
        u = lax.dot_general(k_end, vb, (((0,), (0,)), ((), ())),
                            preferred_element_type=F32)
        decay = jnp.exp(jnp.broadcast_to(b_last, (hk, hk)).T)
        state_ref[hd] = st * jnp.concatenate([decay] * (hv // hk), axis=1) + u
        o = o * lax.rsqrt(jnp.mean(o * o, axis=-1, keepdims=True) + RMS_EPS)
        o = o * ng_ref[...]
        o = o * (g / (1.0 + jnp.exp(-g)))
        og_ref[rows, hd * hv:(hd + 1) * hv] = o.astype(BF16)

    units = [(c, hd) for c in range(n_chunks) for hd in range(GLA_HEADS)]
    pending = scores(*units[0])
    for i, unit in enumerate(units):
        ahead = scores(*units[i + 1]) if i + 1 < len(units) else None
        if i % (len(units) // 8) == 0:
            between_units()
        finish(*unit, *pending)
        pending = ahead

    mix = jnp.dot(og_ref[...], wo_ref[...], preferred_element_type=F32)
    for _ in range(3):
        between_units()
    o_ref[...] = _layer_norm(alpha * hres_ref[...] + mix, lng_ref[...], lnb_ref[...])


def _gla_layer(h, w_in, w_a2, b_a, norm_g, w_o, ln_g, ln_b, alpha):
    bsz, seq, d = h.shape
    dqk = w_a2.shape[1]
    dv = w_o.shape[0]
    n_main = 2 * dqk + 2 * dv
    hv = dv // GLA_HEADS
    hk = dqk // GLA_HEADS
    wqkvg = w_in.astype(BF16)
    wa1 = jnp.pad(w_in[:, n_main:], ((0, 0), (0, LANES - GATE_RANK))).astype(BF16)
    wa2 = jnp.pad(w_a2, ((0, LANES - GATE_RANK), (0, 0))).astype(BF16)
    tri = (jnp.arange(GLA_CHUNK)[:, None] >= jnp.arange(GLA_CHUNK)[None, :]).astype(BF16)
    tri3 = jnp.concatenate([tri, tri, tri], axis=1)
    ts = GLA_TOKENS
    assert seq % ts == 0 and n_main % GLA_PROJ_PIECE == 0 and hv % hk == 0
    n_tiles = bsz * seq // ts
    kern = functools.partial(_gla_kernel, alpha=alpha, dqk=dqk, dv=dv, tiles_per_seq=seq // ts)
    out = pl.pallas_call(
        kern,
        grid=(n_tiles + 1,),
        in_specs=[
            pl.BlockSpec((ts, d), lambda g: (jnp.minimum(g, n_tiles - 1), 0)),
            pl.BlockSpec((ts, d), lambda g: (jnp.maximum(g - 1, 0), 0)),
            _const_spec(w_in.shape),
            _const_spec((d, LANES)),
            _const_spec((LANES, dqk)),
            _const_spec((1, dqk)),
            _const_spec((1, hv)),
            _const_spec((dv, d)),
            _const_spec((1, d)),
            _const_spec((1, d)),
            _const_spec((GLA_CHUNK, 3 * GLA_CHUNK)),
        ],
        out_specs=pl.BlockSpec((ts, d), lambda g: (jnp.maximum(g - 1, 0), 0)),
        out_shape=jax.ShapeDtypeStruct((bsz * seq, d), F32),
        scratch_shapes=[
            pltpu.VMEM((GLA_HEADS, hk, hv), F32),
            pltpu.VMEM((ts, dv), BF16),
            pltpu.VMEM((ts, n_main), F32), pltpu.VMEM((ts, n_main), F32),
            pltpu.VMEM((ts, LANES), F32), pltpu.VMEM((ts, LANES), F32),
            pltpu.VMEM((ts, d), BF16),
        ],
        compiler_params=pltpu.CompilerParams(
            dimension_semantics=("arbitrary",), vmem_limit_bytes=VMEM_LIMIT),
        name="gla_layer",
    )(h.reshape(bsz * seq, d), h.reshape(bsz * seq, d), wqkvg, wa1, wa2, b_a.reshape(1, dqk), norm_g.reshape(1, hv),
      w_o.astype(BF16), ln_g.reshape(1, d), ln_b.reshape(1, d), tri3)
    return out.reshape(bsz, seq, d)


def _mlp_kernel(*refs, alpha, with_mixer, with_q, n_tiles):
    refs = list(refs)
    h_ref = refs.pop(0)
    if with_mixer:
        at_ref, wo_ref, g1_ref, b1_ref = (refs.pop(0) for _ in range(4))
    wup_ref, wdn_ref, g_ref, b_ref = (refs.pop(0) for _ in range(4))
    wq_ref = refs.pop(0) if with_q else None
    o_ref = refs.pop(0)
    q_ref = refs.pop(0) if with_q else None
    pre_refs = (refs.pop(0), refs.pop(0))
    if with_mixer:
        h1_refs = (refs.pop(0), refs.pop(0))
        hb_refs = (refs.pop(0), refs.pop(0))
    g = pl.program_id(0)

    def mlp(hb, anchors):
        acc = None
        for i, f in enumerate(range(0, wup_ref.shape[1], MLP_FF_CHUNK)):
            u = jnp.dot(hb, wup_ref[:, f:f + MLP_FF_CHUNK], preferred_element_type=F32)
            u = jnp.maximum(u, 0.0)
            u = (u * u).astype(BF16)
            if i in anchors:
                u = jnp.where(g >= 0, u, anchors[i])
            dn = jnp.dot(u, wdn_ref[f:f + MLP_FF_CHUNK, :], preferred_element_type=F32)
            acc = dn if acc is None else acc + dn
        return acc

    def step(p, run_a, run_b, run_c):
        anchors = {}
        if with_mixer:
            pre_in, pre_out = pre_refs[p], pre_refs[1 - p]
        else:
            pre_in, pre_out = pre_refs[1 - p], pre_refs[p]
        if run_a:
            mix = lax.dot_general(at_ref[...], wo_ref[...], (((0,), (0,)), ((), ())),
                                  preferred_element_type=F32)
        if run_c:
            out = _layer_norm(pre_in[...], g_ref[...], b_ref[...])
            o_ref[...] = out
            out_b = out.astype(BF16)
            anchors[MLP_ANCHOR_CHUNKS[0]] = out_b
        if run_a:
            h1 = _layer_norm(alpha * h_ref[...] + mix, g1_ref[...], b1_ref[...])
            h1_b = h1.astype(BF16)
            h1_refs[p][...] = h1
            hb_refs[p][...] = h1_b
            anchors[MLP_ANCHOR_CHUNKS[1]] = h1_b
        if run_b and with_mixer:
            pre_out[...] = alpha * h1_refs[1 - p][...] + mlp(hb_refs[1 - p][...], anchors)
        elif run_b:
            h = h_ref[...]
            pre_out[...] = alpha * h + mlp(h.astype(BF16), anchors)
        if run_c and with_q:
            q = jnp.dot(out_b, wq_ref[...], preferred_element_type=F32)
            for hp in range(q_ref.shape[0]):
                q_ref[hp] = q[:, hp * LANES:(hp + 1) * LANES].astype(BF16)

    first_full = 2 if with_mixer else 1
    for p in range(2):
        @pl.when(jnp.logical_and(jnp.logical_and(g >= first_full, g < n_tiles),
                                 lax.rem(g, 2) == p))
        def _(p=p):
            step(p, with_mixer, True, True)

    if with_mixer:
        edge_steps = {0: (True, False, False), 1: (True, True, False),
                      n_tiles: (False, True, True), n_tiles + 1: (False, False, True)}
    else:
        edge_steps = {0: (False, True, False), n_tiles: (False, False, True)}
    for g_edge, stages in edge_steps.items():
        @pl.when(g == g_edge)
        def _(g_edge=g_edge, stages=stages):
            step(g_edge % 2, *stages)


def _mlp_layer(h, w_up, w_down, ln_g, ln_b, alpha, mixer=None, wq=None):
    bsz, seq, d = h.shape
    dff = w_up.shape[1]
    tm = MLP_TOKENS
    assert seq % tm == 0 and dff % MLP_FF_CHUNK == 0 and d == MLP_FF_CHUNK
    tiles_per_seq = seq // tm
    n_tiles = bsz * tiles_per_seq
    lag = 2 if mixer is not None else 1
    assert n_tiles > lag, "the staged MLP kernel needs more token tiles than stages"
    row = lambda v: v.reshape(1, d)
    tile_in = lambda g: jnp.minimum(g, n_tiles - 1)
    tile_out = lambda g: jnp.maximum(g - lag, 0)
    in_specs, args = [pl.BlockSpec((tm, d), lambda g: (tile_in(g), 0))], [h.reshape(bsz * seq, d)]
    if mixer is not None:
        at, w_o, g1, b1 = mixer
        in_specs += [pl.BlockSpec((None, d, tm), lambda g: (tile_in(g) // tiles_per_seq, 0,
                                                            tile_in(g) % tiles_per_seq)),
                     _const_spec((d, d)), _const_spec((1, d)), _const_spec((1, d))]
        args += [at, w_o.astype(BF16), row(g1), row(b1)]
    in_specs += [_const_spec((d, dff)), _const_spec((dff, d)), _const_spec((1, d)), _const_spec((1, d))]
    args += [w_up.astype(BF16), w_down.astype(BF16), row(ln_g), row(ln_b)]
    out_specs = [pl.BlockSpec((tm, d), lambda g: (tile_out(g), 0))]
    out_shape = [jax.ShapeDtypeStruct((bsz * seq, d), F32)]
    if wq is not None:
        in_specs.append(_const_spec((d, d)))
        args.append(wq)
        out_specs.append(pl.BlockSpec((None, d // LANES, tm, LANES),
                                      lambda g: (tile_out(g) // tiles_per_seq, 0,
                                                 tile_out(g) % tiles_per_seq, 0)))
        out_shape.append(jax.ShapeDtypeStruct((bsz, d // LANES, seq, LANES), BF16))
    scratch = [pltpu.VMEM((tm, d), F32)] * 2
    if mixer is not None:
        scratch += [pltpu.VMEM((tm, d), F32)] * 2
        scratch += [pltpu.VMEM((tm, d), BF16)] * 2
    outs = pl.pallas_call(
        functools.partial(_mlp_kernel, alpha=alpha, with_mixer=mixer is not None,
                          with_q=wq is not None, n_tiles=n_tiles),
        grid=(n_tiles + lag,),
        in_specs=in_specs,
        out_specs=out_specs,
        out_shape=out_shape,
        scratch_shapes=scratch,
        compiler_params=pltpu.CompilerParams(
            dimension_semantics=("arbitrary",), vmem_limit_bytes=VMEM_LIMIT),
        name="mlp_layer",
    )(*args)
    return [outs[0].reshape(bsz, seq, d)] + list(outs[1:])


def _sb_proj_kernel(h_ref, w_ref, *refs, with_vt):
    hb = h_ref[...].astype(BF16)
    d = h_ref.shape[1]
    res = jnp.dot(hb, w_ref[...], preferred_element_type=F32)
    out_refs = refs
    if with_vt:
        wvt_ref, out_refs, vt_ref = refs[0], refs[1:-1], refs[-1]
        vt = lax.dot_general(wvt_ref[...], hb, (((1,), (1,)), ((), ())), preferred_element_type=F32)
        vt_ref[...] = vt.astype(BF16)
    for n, o_ref in enumerate(out_refs):
        for hp in range(d // LANES):
            c0 = n * d + hp * LANES
            o_ref[hp] = res[:, c0:c0 + LANES].astype(BF16)


def _sb_proj(h, w, wvt=None):
    bsz, seq, d = h.shape
    n_out = w.shape[1] // d
    tm = PROJ_TOKENS
    assert seq % tm == 0 and d % LANES == 0
    hp = d // LANES
    out_sds = [jax.ShapeDtypeStruct((bsz, hp, seq, LANES), BF16)] * n_out
    out_specs = [pl.BlockSpec((None, hp, tm, LANES), lambda b, s: (b, 0, s, 0))] * n_out
    in_specs = [pl.BlockSpec((None, tm, d), lambda b, s: (b, s, 0)), _const_spec((d, n_out * d))]
    args = [h, w]
    if wvt is not None:
        in_specs.append(_const_spec((d, d)))
        args.append(wvt)
        out_sds.append(jax.ShapeDtypeStruct((bsz, d, seq), BF16))
        out_specs.append(pl.BlockSpec((None, d, tm), lambda b, s: (b, 0, s)))
    return pl.pallas_call(
        functools.partial(_sb_proj_kernel, with_vt=wvt is not None),
        grid=(bsz, seq // tm),
        in_specs=in_specs,
        out_specs=out_specs,
        out_shape=out_sds,
        compiler_params=pltpu.CompilerParams(
            dimension_semantics=("arbitrary", "arbitrary"), vmem_limit_bytes=VMEM_LIMIT),
        name="sb_proj",
    )(*args)


def _sb_logits(ks, q_blk, bias):
    z = lax.dot_general(ks, q_blk, (((1,), (1,)), ((), ())), preferred_element_type=F32)

    def softplus(x):
        return jnp.maximum(x, jnp.log(1.0 + jnp.exp(jnp.minimum(x, SB_EXP_CLAMP)))).astype(BF16)

    if bias is None:
        return z, softplus(z)
    half = SB_T // 2
    zs, ts = [], []
    for hd in range(2):
        r0 = hd * SB_T
        za = z[r0:r0 + half, :half] + bias[r0:r0 + half, :half]
        zb = z[r0:r0 + SB_T, half:] + bias[r0:r0 + SB_T, half:]
        ta = jnp.concatenate([softplus(za), jnp.zeros((half, half), BF16)], axis=0)
        zs.append((za, zb))
        ts.append(jnp.concatenate([ta, softplus(zb)], axis=1))
    return zs, jnp.concatenate(ts, axis=0)


def _sb_weights(z, tb, aa_ref, r):
    half = SB_T // 2
    ws, rs = [], []
    for hd in range(2):
        rows = slice(hd * SB_T, (hd + 1) * SB_T)
        cs = jnp.dot(aa_ref[...], tb[rows], preferred_element_type=F32)
        if r is None:
            za, zb = z[hd]
            wa = jnp.exp(za - cs[:half, :half]).astype(BF16)
            wb = jnp.exp(zb - cs[:SB_T, half:]).astype(BF16)
            wa = jnp.concatenate([wa, jnp.zeros((half, half), BF16)], axis=0)
            ws.append(jnp.concatenate([wa, wb], axis=1))
        else:
            ws.append(jnp.exp(z[rows] - cs[:SB_T] - r[hd]).astype(BF16))
        rs.append(cs[SB_T:SB_T + 1])
    return jnp.concatenate(ws, axis=0), rs


def _sb_values(vt2, w):
    return jnp.dot(vt2, w, preferred_element_type=F32)


def _sb_tile(ks, q_blk, vt2, aa_ref, bias, r):
    z, tb = _sb_logits(ks, q_blk, bias)
    w, rs = _sb_weights(z, tb, aa_ref, r)
    return _sb_values(vt2, w), rs


def _sb_attn_kernel(q_ref, k_ref, vt_ref, aa_ref, bias_ref, o_ref, acc_ref, r_ref):
    n_blocks = q_ref.shape[0] // SB_T
    lane_k = lax.broadcasted_iota(jnp.int32, (SB_T, LANES), 1)
    row_v = lax.broadcasted_iota(jnp.int32, (LANES, SB_T), 0)

    def key_tile(k0):
        k = k_ref[pl.ds(k0, SB_T), :]
        vt = vt_ref[:, pl.ds(k0, SB_T)]
        zk, zv = jnp.zeros_like(k), jnp.zeros_like(vt)
        ks = jnp.concatenate([jnp.where(lane_k < SB_HEAD_DIM, k, zk),
                              jnp.where(lane_k >= SB_HEAD_DIM, k, zk)], axis=0)
        vt2 = jnp.concatenate([jnp.where(row_v < SB_HEAD_DIM, vt, zv),
                               jnp.where(row_v >= SB_HEAD_DIM, vt, zv)], axis=1)
        return ks, vt2

    def tile(q_blk, kv, bias, r):
        return _sb_tile(kv[0], q_blk, kv[1], aa_ref, bias, r)

    kvs = [key_tile(m * SB_T) for m in range(n_blocks)]
    tiles = [(m, j) for m in range(n_blocks) for j in ((m, m - 1) if m else (m,))]
    logits, weights, rsums, accs = {}, {}, {}, {}
    lowest_sums = None
    for i in range(len(tiles) + 2):
        if i < len(tiles):
            m, j = tiles[i]
            logits[i] = _sb_logits(kvs[j][0], q_ref[m * SB_T:(m + 1) * SB_T, :],
                                   bias_ref[...] if j == m else None)
        if 0 <= i - 1 < len(tiles):
            m, j = tiles[i - 1]
            weights[i - 1], rs = _sb_weights(*logits.pop(i - 1), aa_ref, rsums.get(m))
            rsums[m] = [a + b for a, b in zip(rsums[m], rs)] if m in rsums else rs
        if 0 <= i - 2 < len(tiles):
            m, j = tiles[i - 2]
            pv = _sb_values(kvs[j][1], weights.pop(i - 2))
            accs[m] = accs[m] + pv if m in accs else pv
            if j == max(m - 1, 0):
                o_ref[:, m * SB_T:(m + 1) * SB_T] = accs[m].astype(o_ref.dtype)
                if m > 1:
                    acc_ref[m] = accs[m]
                    r_ref[m] = jnp.concatenate(rsums[m], axis=0)
                    lowest_sums = rsums[m] if lowest_sums is None else [
                        jnp.minimum(a, b) for a, b in zip(lowest_sums, rsums[m])]

    def continue_block(m, carry):
        q0 = pl.multiple_of(m * SB_T, SB_T)

        def more(c):
            j, _, _, lowest = c
            return jnp.logical_and(j >= 0, lowest < SB_LOG_WEIGHT_FLOOR)

        def step(c):
            j, acc, rsum, _ = c
            pv, rs = tile(q_ref[pl.ds(q0, SB_T), :], key_tile(pl.multiple_of(j * SB_T, SB_T)),
                          None, [rsum[0:1], rsum[1:2]])
            rsum = rsum + jnp.concatenate(rs, axis=0)
            return j - 1, acc + pv, rsum, jnp.min(rsum)

        rsum = r_ref[m]
        lowest = jnp.min(rsum)

        @pl.when(lowest < SB_LOG_WEIGHT_FLOOR)
        def _():
            _, acc, _, _ = lax.while_loop(more, step, (m - 2, acc_ref[m], rsum, lowest))
            o_ref[:, pl.ds(q0, SB_T)] = acc.astype(o_ref.dtype)

        return carry

    if lowest_sums is not None:
        @pl.when(jnp.min(jnp.minimum(*lowest_sums)) < SB_LOG_WEIGHT_FLOOR)
        def _():
            lax.fori_loop(2, n_blocks, continue_block, 0)


def _sb_attention(q, k, vt):
    bsz, hp, seq, _ = q.shape
    assert seq % SB_T == 0
    ar = jnp.arange(SB_T)
    later = (ar[None, :] >= ar[:, None]).astype(BF16)
    aa = jnp.concatenate([later, jnp.ones((SB_SUM_ROWS, SB_T), BF16)], axis=0)
    causal = ar[:, None] < ar[None, :]
    bias = jnp.where(jnp.concatenate([causal, causal], axis=0), 0.0, SB_MASK_VALUE).astype(F32)
    slab = pl.BlockSpec((None, None, seq, LANES), lambda b, p: (b, p, 0, 0))
    slab_t = pl.BlockSpec((None, LANES, seq), lambda b, p: (b, p, 0))
    return pl.pallas_call(
        _sb_attn_kernel,
        grid=(bsz, hp),
        in_specs=[slab, slab, slab_t, _const_spec((SB_T + SB_SUM_ROWS, SB_T)),
                  _const_spec((2 * SB_T, SB_T))],
        out_specs=slab_t,
        out_shape=jax.ShapeDtypeStruct((bsz, hp * LANES, seq), BF16),
        scratch_shapes=[
            pltpu.VMEM((seq // SB_T, LANES, SB_T), F32),
            pltpu.VMEM((seq // SB_T, 2, SB_T), F32),
        ],
        compiler_params=pltpu.CompilerParams(
            dimension_semantics=("arbitrary", "arbitrary"), vmem_limit_bytes=VMEM_LIMIT),
        name="sb_attention",
    )(q, k, vt, aa, bias)


def kernel(x, gla_w_in, gla_w_a2, gla_b_a, gla_norm_g, gla_w_o, kv_w, sb_w_q, sb_w_o,
           mlp_w_up, mlp_w_down, ln_g, ln_b):
    bsz, seq, d = x.shape
    depth = mlp_w_up.shape[0]
    n_a = gla_w_in.shape[0]
    alpha = (2.0 * depth) ** 0.25
    q_scale = SB_HEAD_DIM ** -0.5

    h = x
    q = k_sh = vt_sh = None
    for layer in range(depth):
        mixer = None
        if layer < n_a:
            h = _gla_layer(h, gla_w_in[layer], gla_w_a2[layer], gla_b_a[layer], gla_norm_g[layer],
                           gla_w_o[layer], ln_g[layer, 0], ln_b[layer, 0], alpha)
        else:
            j = layer - n_a
            if j == 0:
                wq = (sb_w_q[j] * q_scale).astype(BF16)
                wqk = jnp.concatenate([wq, kv_w[:, :d].astype(BF16)], axis=1)
                q, k_sh, vt_sh = _sb_proj(h, wqk, kv_w[:, d:].T.astype(BF16))
            mixer = (_sb_attention(q, k_sh, vt_sh), sb_w_o[j], ln_g[layer, 0], ln_b[layer, 0])
        next_wq = None
        if n_a <= layer < depth - 1:
            next_wq = (sb_w_q[layer + 1 - n_a] * q_scale).astype(BF16)
        outs = _mlp_layer(h, mlp_w_up[layer], mlp_w_down[layer], ln_g[layer, 1], ln_b[layer, 1],
                          alpha, mixer=mixer, wq=next_wq)
        h = outs[0]
        if next_wq is not None:
            q = outs[1]
    return h
```
